```python
import jax
import jax.numpy as jnp
from jax import lax
import numpy as np

D_MODEL = 1024
BATCH = 4
SEQ = 4096
DEPTH = 4

GRID_W = 64
CTX_LEN = 256
N_MIXERS = 4
EPS = 1e-6
N_MOD = 6

FNET_GROUPS = 4
FNET_GROUP_DIM = D_MODEL // FNET_GROUPS

NA_HEADS = 16
NA_HEAD_DIM = D_MODEL // NA_HEADS
NA_WIN_ROWS = 8
NA_WIN_COLS = 16

SG_CHUNK = 128
SG_GROUPS = 4
SG_DIM = D_MODEL
SG_GROUP_DIM = SG_DIM // SG_GROUPS

ATT_HEADS = 16
ATT_KV_HEADS = 4
ATT_GROUP = ATT_HEADS // ATT_KV_HEADS
ATT_HEAD_DIM = D_MODEL // ATT_HEADS
ATT_Q_BLOCK = 128
ROPE_THETA = 10000.0

FFN_DIM = 2816
FFN_CONV = 3

kernel_name = "hybrid_interleaved_diffusion_trunk"


def _layers_using(m):
    return (DEPTH - m + N_MIXERS - 1) // N_MIXERS


def rms_norm(x, g):
    xf = x.astype(jnp.float32)
    y = xf * lax.rsqrt(jnp.mean(xf * xf, axis=-1, keepdims=True) + EPS)
    return (y * g.astype(jnp.float32)).astype(x.dtype)


def modulate(h, g, shift, scale):
    return rms_norm(h, g) * (1 + scale) + shift


def ada_params(cond, w_mod, b_mod):
    m = jax.nn.silu(cond) @ w_mod + b_mod
    return jnp.split(m[:, None, :], N_MOD, axis=-1)


def depthwise_conv_seq(h, w, b):
    pad = FFN_CONV // 2
    y = lax.conv_general_dilated(h, w[:, None, :].astype(h.dtype), window_strides=(1,),
                                 padding=((pad, FFN_CONV - 1 - pad),),
                                 dimension_numbers=("NWC", "WIO", "NWC"),
                                 feature_group_count=h.shape[-1])
    return y + b


def conv_ffn(h, w_up, w_conv, b_conv, w_down):
    g, v = jnp.split(h @ w_up, 2, axis=-1)
    g = depthwise_conv_seq(g, w_conv, b_conv)
    return (jax.nn.silu(g) * v) @ w_down


def fourier_mix(a, w_out):
    B, L, D = a.shape
    ag = a.reshape(B, L, FNET_GROUPS, FNET_GROUP_DIM).astype(jnp.float32)
    f = jnp.fft.fft2(ag, axes=(1, 3), norm="ortho").real
    return f.astype(a.dtype).reshape(B, L, D) @ w_out


def spatial_gating_mix(a, w_in, g_v, w_s, b_s, w_out):
    B, L, _ = a.shape
    z = jax.nn.gelu(a @ w_in)
    u, v = jnp.split(z, 2, axis=-1)
    v = rms_norm(v, g_v)
    n = L // SG_CHUNK
    vg = v.reshape(B, n, SG_CHUNK, SG_GROUPS, SG_GROUP_DIM)
    mixed = jnp.einsum("gpq,bnqgc->bnpgc", w_s, vg) + b_s.T[None, None, :, :, None]
    return (u * mixed.reshape(B, L, SG_DIM)) @ w_out


def axial_rope(L):
    t = jnp.arange(L)
    row = (t // GRID_W).astype(jnp.float32)
    col = (t % GRID_W).astype(jnp.float32)
    n_freq = ATT_HEAD_DIM // 4
    inv_freq = ROPE_THETA ** (-jnp.arange(n_freq, dtype=jnp.float32) / n_freq)
    ang = jnp.concatenate([row[:, None] * inv_freq, col[:, None] * inv_freq], axis=-1)
    return jnp.cos(ang), jnp.sin(ang)


def apply_rope(x, cos, sin):
    xf = x.astype(jnp.float32)
    x1, x2 = jnp.split(xf, 2, axis=-1)
    c = cos[None, :, None, :]
    s = sin[None, :, None, :]
    return jnp.concatenate([x1 * c - x2 * s, x1 * s + x2 * c], axis=-1).astype(x.dtype)


def _gqa_attend(q, k, v):
    s = jnp.einsum("bqkgd,bskd->bkgqs", q, k).astype(jnp.float32) * (ATT_HEAD_DIM ** -0.5)
    p = jax.nn.softmax(s, axis=-1).astype(v.dtype)
    return jnp.einsum("bkgqs,bskd->bqkgd", p, v)


def gqa_mix(a_lat, a_ctx, w_qkv, g_q, g_k, w_out, need_ctx_out):
    B, L, _ = a_lat.shape
    H, KV, G, Dh = ATT_HEADS, ATT_KV_HEADS, ATT_GROUP, ATT_HEAD_DIM
    q, k, v = jnp.split(a_lat @ w_qkv, [H * Dh, (H + KV) * Dh], axis=-1)
    q = rms_norm(q.reshape(B, L, H, Dh), g_q)
    k = rms_norm(k.reshape(B, L, KV, Dh), g_k)
    v = v.reshape(B, L, KV, Dh)
    cos, sin = axial_rope(L)
    q = apply_rope(q, cos, sin)
    k = apply_rope(k, cos, sin)
    n_ctx = a_ctx.shape[1]
    kc, vc = jnp.split(a_ctx @ w_qkv[:, H * Dh:], 2, axis=-1)
    kc = rms_norm(kc.reshape(B, n_ctx, KV, Dh), g_k)
    vc = vc.reshape(B, n_ctx, KV, Dh)
    k_all = jnp.concatenate([k, kc], axis=1)
    v_all = jnp.concatenate([v, vc], axis=1)
    nblk = L // ATT_Q_BLOCK
    q_blocks = q.reshape(B, nblk, ATT_Q_BLOCK, KV, G, Dh).swapaxes(0, 1)
    o = lax.map(lambda qb: _gqa_attend(qb, k_all, v_all), q_blocks)
    y_lat = o.swapaxes(0, 1).reshape(B, L, H * Dh) @ w_out
    y_ctx = None
    if need_ctx_out:
        qc = rms_norm((a_ctx @ w_qkv[:, :H * Dh]).reshape(B, n_ctx, H, Dh), g_q)
        oc = _gqa_attend(qc.reshape(B, n_ctx, KV, G, Dh), kc, vc)
        y_ctx = oc.reshape(B, n_ctx, H * Dh) @ w_out
    return y_lat, y_ctx


def neighbourhood_mix(a_lat, a_ctx, w_qkv, rpb, w_out, need_ctx_out):
    B, L, D = a_lat.shape
    H, Dh = NA_HEADS, NA_HEAD_DIM
    rows = L // GRID_W
    kh = min(NA_WIN_ROWS, rows)
    kw = NA_WIN_COLS
    scale = Dh ** -0.5
    q, k, v = jnp.split(a_lat @ w_qkv, 3, axis=-1)
    n_ctx = a_ctx.shape[1]
    kc, vc = jnp.split(a_ctx @ w_qkv[:, D:], 2, axis=-1)
    kc = kc.reshape(B, n_ctx, H, Dh)
    vc = vc.reshape(B, n_ctx, H, Dh)
    k_grid = k.reshape(B, rows, GRID_W, H, Dh)
    v_grid = v.reshape(B, rows, GRID_W, H, Dh)
    col = jnp.arange(GRID_W)
    col_start = jnp.clip(col - kw // 2, 0, GRID_W - kw)
    col_idx = col_start[:, None] + jnp.arange(kw)[None, :]
    col_off = col_idx - col[:, None] + (NA_WIN_COLS - 1)
    q_rows = q.reshape(B, rows, GRID_W, H, Dh).swapaxes(0, 1)

    def row_block(args):
        r, q_r = args
        r_start = jnp.clip(r - kh // 2, 0, rows - kh)
        k_r = lax.dynamic_slice_in_dim(k_grid, r_start, kh, axis=1)[:, :, col_idx]
        v_r = lax.dynamic_slice_in_dim(v_grid, r_start, kh, axis=1)[:, :, col_idx]
        row_off = r_start + jnp.arange(kh) - r + (NA_WIN_ROWS - 1)
        bias = rpb[:, row_off[:, None, None], col_off[None, :, :]].astype(jnp.float32)
        s_nb = jnp.einsum("bwhd,bawjhd->bhwaj", q_r, k_r).astype(jnp.float32) * scale
        s_nb = (s_nb + bias.transpose(0, 2, 1, 3)[None]).reshape(B, H, GRID_W, kh * kw)
        s_ctx = jnp.einsum("bwhd,bchd->bhwc", q_r, kc).astype(jnp.float32) * scale
        p = jax.nn.softmax(jnp.concatenate([s_nb, s_ctx], axis=-1), axis=-1).astype(q_r.dtype)
        p_nb = p[..., :kh * kw].reshape(B, H, GRID_W, kh, kw)
        p_ctx = p[..., kh * kw:]
        return (jnp.einsum("bhwaj,bawjhd->bwhd", p_nb, v_r)
                + jnp.einsum("bhwc,bchd->bwhd", p_ctx, vc))

    o = lax.map(row_block, (jnp.arange(rows), q_rows))
    y_lat = o.swapaxes(0, 1).reshape(B, L, D) @ w_out
    y_ctx = None
    if need_ctx_out:
        qc = (a_ctx @ w_qkv[:, :D]).reshape(B, n_ctx, H, Dh)
        s = jnp.einsum("bqhd,bkhd->bhqk", qc, kc).astype(jnp.float32) * scale
        p = jax.nn.softmax(s, axis=-1).astype(vc.dtype)
        y_ctx = jnp.einsum("bhqk,bkhd->bqhd", p, vc).reshape(B, n_ctx, D) @ w_out
    return y_lat, y_ctx


def _normal(k, shape, scale):
    return jax.random.normal(k, shape, jnp.float32) * scale


def setup_inputs(seed: int = 0) -> dict:
    key = jax.random.key(seed)
    ks = iter(jax.random.split(key, 32))
    D = D_MODEL
    inv = D ** -0.5
    n_a, n_b, n_c, n_d = (_layers_using(m) for m in range(N_MIXERS))
    qkv_dim = (ATT_HEADS + 2 * ATT_KV_HEADS) * ATT_HEAD_DIM
    return {
        "x": _normal(next(ks), (BATCH, SEQ, D), 1.0),
        "c": _normal(next(ks), (BATCH, D), 1.0),
        "ctx": _normal(next(ks), (BATCH, CTX_LEN, D), 1.0),
        "c_ctx": _normal(next(ks), (D,), 1.0),
        "w_mod": _normal(next(ks), (DEPTH, D, N_MOD * D), 0.5 * inv),
        "b_mod": _normal(next(ks), (DEPTH, N_MOD * D), 0.01),
        "g_norm_mix": 1.0 + _normal(next(ks), (DEPTH, D), 0.02),
        "g_norm_ffn": 1.0 + _normal(next(ks), (DEPTH, D), 0.02),
        "w_ffn_up": _normal(next(ks), (DEPTH, D, 2 * FFN_DIM), inv),
        "w_ffn_conv": _normal(next(ks), (DEPTH, FFN_CONV, FFN_DIM), FFN_CONV ** -0.5),
        "b_ffn_conv": _normal(next(ks), (DEPTH, FFN_DIM), 0.01),
        "w_ffn_down": _normal(next(ks), (DEPTH, FFN_DIM, D), FFN_DIM ** -0.5),
        "w_fnet_out": _normal(next(ks), (n_a, D, D), inv),
        "w_na_qkv": _normal(next(ks), (n_b, D, 3 * D), inv),
        "na_rel_bias": _normal(next(ks), (n_b, NA_HEADS, 2 * NA_WIN_ROWS - 1, 2 * NA_WIN_COLS - 1), 0.1),
        "w_na_out": _normal(next(ks), (n_b, D, D), inv),
        "w_sg_in": _normal(next(ks), (n_c, D, 2 * SG_DIM), inv),
        "g_sg_v": 1.0 + _normal(next(ks), (n_c, SG_DIM), 0.02),
        "w_sg_spatial": _normal(next(ks), (n_c, SG_GROUPS, SG_CHUNK, SG_CHUNK), SG_CHUNK ** -0.5),
        "b_sg_spatial": 1.0 + _normal(next(ks), (n_c, SG_GROUPS, SG_CHUNK), 0.01),
        "w_sg_out": _normal(next(ks), (n_c, SG_DIM, D), SG_DIM ** -0.5),
        "w_att_qkv": _normal(next(ks), (n_d, D, qkv_dim), inv),
        "g_att_q": 1.0 + _normal(next(ks), (n_d, ATT_HEAD_DIM), 0.02),
        "g_att_k": 1.0 + _normal(next(ks), (n_d, ATT_HEAD_DIM), 0.02),
        "w_att_out": _normal(next(ks), (n_d, ATT_HEADS * ATT_HEAD_DIM, D), (ATT_HEADS * ATT_HEAD_DIM) ** -0.5),
        "g_final": 1.0 + _normal(next(ks), (D,), 0.02),
    }


def reference(x, c, ctx, c_ctx, w_mod, b_mod, g_norm_mix, g_norm_ffn, w_ffn_up, w_ffn_conv, b_ffn_conv,
              w_ffn_down, w_fnet_out, w_na_qkv, na_rel_bias, w_na_out, w_sg_in, g_sg_v, w_sg_spatial,
              b_sg_spatial, w_sg_out, w_att_qkv, g_att_q, g_att_k, w_att_out, g_final):
    h_lat, h_ctx = x, ctx
    for i in range(DEPTH):
        m, j = i % N_MIXERS, i // N_MIXERS
        last = i == DEPTH - 1
        ctx_in_needed = (not last) or m in (1, 3)
        sh1, sc1, gt1, sh2, sc2, gt2 = ada_params(c, w_mod[i], b_mod[i])
        a_lat = modulate(h_lat, g_norm_mix[i], sh1, sc1)
        if ctx_in_needed:
            csh1, csc1, cgt1, csh2, csc2, cgt2 = ada_params(c_ctx[None, :], w_mod[i], b_mod[i])
            a_ctx = modulate(h_ctx, g_norm_mix[i], csh1, csc1)
        if m == 0:
            y_lat = fourier_mix(a_lat, w_fnet_out[j])
            y_ctx = None if last else fourier_mix(a_ctx, w_fnet_out[j])
        elif m == 1:
            y_lat, y_ctx = neighbourhood_mix(a_lat, a_ctx, w_na_qkv[j], na_rel_bias[j], w_na_out[j], not last)
        elif m == 2:
            y_lat = spatial_gating_mix(a_lat, w_sg_in[j], g_sg_v[j], w_sg_spatial[j], b_sg_spatial[j], w_sg_out[j])
            y_ctx = None if last else spatial_gating_mix(a_ctx, w_sg_in[j], g_sg_v[j], w_sg_spatial[j],
                                                         b_sg_spatial[j], w_sg_out[j])
        else:
            y_lat, y_ctx = gqa_mix(a_lat, a_ctx, w_att_qkv[j], g_att_q[j], g_att_k[j], w_att_out[j], not last)
        h_lat = h_lat + gt1 * y_lat
        h_lat = h_lat + gt2 * conv_ffn(modulate(h_lat, g_norm_ffn[i], sh2, sc2),
                                       w_ffn_up[i], w_ffn_conv[i], b_ffn_conv[i], w_ffn_down[i])
        if not last:
            h_ctx = h_ctx + cgt1 * y_ctx
            h_ctx = h_ctx + cgt2 * conv_ffn(modulate(h_ctx, g_norm_ffn[i], csh2, csc2),
                                           w_ffn_up[i], w_ffn_conv[i], b_ffn_conv[i], w_ffn_down[i])
    return rms_norm(h_lat, g_final)
```

```python
import functools

import numpy as np
import jax
import jax.numpy as jnp
from jax import lax
from jax.experimental import pallas as pl
from jax.experimental.pallas import tpu as pltpu

F32 = jnp.float32
BF16 = jnp.bfloat16

D = 1024
B = 4
L = 4096
NCTX = 256
DEPTH = 4
GRID_W = 64
EPS = 1e-6
N_MOD = 6
T_LAT = B * L
T_CTX = B * NCTX
T = T_LAT + T_CTX

FNET_GROUPS = 4
FNET_GD = D // FNET_GROUPS

NA_HEADS = 16
NA_DH = D // NA_HEADS
NA_WIN_ROWS = 8
NA_WIN_COLS = 16
NA_QROWS = 4
NA_KROWS = 12
NA_TQ = NA_QROWS * GRID_W
NA_TK = NA_KROWS * GRID_W

SG_CHUNK = 128
SG_GROUPS = 4
SG_GD = D // SG_GROUPS

ATT_HEADS = 16
ATT_KV = 4
ATT_DH = 64
ROPE_THETA = 10000.0
GQ_TQ = 128
GQ_CK = 512

FFN_DIM = 2816
FFN_TF = 256
FFN_NF = FFN_DIM // FFN_TF
HALO = 16

TM = 1024
SG_TM = 512
LANES = 128
NEG = -1e30

VMEM_LIMIT = 56 * 1024 * 1024


def _cp(sem, vmem=VMEM_LIMIT):
    return pltpu.CompilerParams(dimension_semantics=sem, vmem_limit_bytes=vmem)


def _mod_row(i, tm):
    return (i * tm) // L


def _mod_spec(k, tm):
    return pl.BlockSpec((None, 1, D), lambda i: (_mod_row(i, tm) * N_MOD + k, 0, 0))


def _const_spec(shape):
    nd = len(shape)
    return pl.BlockSpec(shape, lambda *_: (0,) * nd)


def _modulate(x, g, shift, scale):
    ms = jnp.mean(x * x, axis=-1, keepdims=True)
    return x * lax.rsqrt(ms + EPS) * (g * (1.0 + scale)) + shift


def _dot(a, b):
    return jnp.dot(a, b, preferred_element_type=F32)


def _dot_nt(a, b):
    return lax.dot_general(a, b, (((1,), (1,)), ((), ())), preferred_element_type=F32)


def _ada_kernel(c_ref, w_ref, b_ref, o_ref):
    c = c_ref[...]
    s = (c * jax.nn.sigmoid(c)).astype(BF16)
    o_ref[0] = _dot(s, w_ref[0].astype(BF16)) + b_ref[0]


def _ada(cond8, w_mod, b_mod):
    tn = 2048
    n = N_MOD * D
    return pl.pallas_call(
        _ada_kernel,
        out_shape=jax.ShapeDtypeStruct((DEPTH, 8, n), F32),
        grid=(DEPTH, n // tn),
        in_specs=[
            pl.BlockSpec((8, D), lambda l, j: (0, 0)),
            pl.BlockSpec((1, D, tn), lambda l, j: (l, 0, j)),
            pl.BlockSpec((1, 1, tn), lambda l, j: (l, 0, j)),
        ],
        out_specs=pl.BlockSpec((1, 8, tn), lambda l, j: (l, 0, j)),
        compiler_params=_cp(("arbitrary", "arbitrary")),
        name="ada_params",
    )(cond8, w_mod, b_mod.reshape(DEPTH, 1, n))


def _ffn_kernel(hp_ref, h_ref, hn_ref, g_ref, sh_ref, sc_ref, gt_ref, wg_ref, wv_ref, wc_ref, bc_ref,
                wd_ref, gf_ref, o_ref, a_ref, *, final_norm):
    i = pl.program_id(0)
    g = g_ref[...]
    sh = sh_ref[...]
    sc = sc_ref[...]

    a_ref[0:HALO, :] = _modulate(hp_ref[...], g, sh, sc).astype(BF16)
    a_ref[HALO:HALO + TM, :] = _modulate(h_ref[...], g, sh, sc).astype(BF16)
    a_ref[HALO + TM:, :] = _modulate(hn_ref[...], g, sh, sc).astype(BF16)

    seq = jnp.where(i * TM >= T_LAT, NCTX, L)
    pos = (i * TM + lax.broadcasted_iota(jnp.int32, (TM, 1), 0)) & (seq - 1)
    m_prev = (pos != 0).astype(F32)
    m_next = (pos != seq - 1).astype(F32)

    for j in range(FFN_NF):
        ge = _dot(a_ref[...], wg_ref[j])
        v = _dot(a_ref[HALO:HALO + TM, :], wv_ref[j])
        wc = wc_ref[j]
        gc = (ge[HALO:HALO + TM] * wc[1:2]
              + (ge[HALO - 1:HALO - 1 + TM] * m_prev) * wc[0:1]
              + (ge[HALO + 1:HALO + 1 + TM] * m_next) * wc[2:3]
              + bc_ref[j])
        u = (gc * jax.nn.sigmoid(gc) * v).astype(BF16)
        y = _dot(u, wd_ref[j])
        if j == 0:
            o_ref[...] = y
        else:
            o_ref[...] += y

    out = h_ref[...] + gt_ref[...] * o_ref[...]
    if final_norm:
        ms = jnp.mean(out * out, axis=-1, keepdims=True)
        out = out * lax.rsqrt(ms + EPS) * gf_ref[...]
    o_ref[...] = out


def _ffn(h, mods, g_norm, wg, wv, wc, bc, wd, g_final, *, n_blocks, final_norm):
    hb = TM // HALO
    last_halo = T // HALO - 1
    kern = functools.partial(_ffn_kernel, final_norm=final_norm)
    return pl.pallas_call(
        kern,
        out_shape=jax.ShapeDtypeStruct((n_blocks * TM, D), F32),
        grid=(n_blocks,),
        in_specs=[
            pl.BlockSpec((HALO, D), lambda i: (jnp.maximum(i * hb - 1, 0), 0)),
            pl.BlockSpec((TM, D), lambda i: (i, 0)),
            pl.BlockSpec((HALO, D), lambda i: (jnp.minimum((i + 1) * hb, last_halo), 0)),
            _const_spec((1, D)),
            _mod_spec(3, TM), _mod_spec(4, TM), _mod_spec(5, TM),
            _const_spec((FFN_NF, D, FFN_TF)),
            _const_spec((FFN_NF, D, FFN_TF)),
            _const_spec((FFN_NF, 3, FFN_TF)),
            _const_spec((FFN_NF, 1, FFN_TF)),
            _const_spec((FFN_NF, FFN_TF, D)),
            _const_spec((1, D)),
        ],
        out_specs=pl.BlockSpec((TM, D), lambda i: (i, 0)),
        scratch_shapes=[pltpu.VMEM((TM + 2 * HALO, D), BF16)],
        compiler_params=_cp(("arbitrary",)),
        name="conv_ffn",
    )(h, h, h, g_norm, mods, mods, mods, wg, wv, wc, bc, wd, g_final)


def _modmm_kernel(h_ref, g_ref, sh_ref, sc_ref, w_ref, o_ref, *, tn):
    a = _modulate(h_ref[...], g_ref[...], sh_ref[...], sc_ref[...]).astype(BF16)
    for c in range(w_ref.shape[1] // tn):
        o_ref[:, c * tn:(c + 1) * tn] = _dot(a, w_ref[:, c * tn:(c + 1) * tn]).astype(BF16)


def _modmm(h, mods, g_norm, w, *, name):
    n = w.shape[1]
    return pl.pallas_call(
        functools.partial(_modmm_kernel, tn=512),
        out_shape=jax.ShapeDtypeStruct((T, n), BF16),
        grid=(T // TM,),
        in_specs=[
            pl.BlockSpec((TM, D), lambda i: (i, 0)),
            _const_spec((1, D)),
            _mod_spec(0, TM), _mod_spec(1, TM),
            _const_spec((D, n)),
        ],
        out_specs=pl.BlockSpec((TM, n), lambda i: (i, 0)),
        compiler_params=_cp(("arbitrary",)),
        name=name,
    )(h, g_norm, mods, mods, w)


def _outproj_kernel(h_ref, x_ref, gt_ref, w_ref, o_ref):
    o_ref[...] = h_ref[...] + gt_ref[...] * _dot(x_ref[...], w_ref[...])


def _outproj(h, x, mods, w, *, n_blocks, name):
    return pl.pallas_call(
        _outproj_kernel,
        out_shape=jax.ShapeDtypeStruct((T, D), F32),
        grid=(n_blocks,),
        in_specs=[
            pl.BlockSpec((TM, D), lambda i: (i, 0)),
            pl.BlockSpec((TM, D), lambda i: (i, 0)),
            _mod_spec(2, TM),
            _const_spec((D, D)),
        ],
        out_specs=pl.BlockSpec((TM, D), lambda i: (i, 0)),
        input_output_aliases={0: 0},
        compiler_params=_cp(("arbitrary",)),
        name=name,
    )(h, x, mods, w)


def _fnet_chan_kernel(h_ref, g_ref, sh_ref, sc_ref, cs_ref, o_ref):
    a = _modulate(h_ref[...], g_ref[...], sh_ref[...], sc_ref[...]).astype(BF16)
    cs = cs_ref[...]
    for grp in range(FNET_GROUPS):
        pq = _dot(a[:, grp * FNET_GD:(grp + 1) * FNET_GD], cs)
        o_ref[0, :, grp * FNET_GD:(grp + 1) * FNET_GD] = pq[:, :FNET_GD].astype(BF16)
        o_ref[1, :, grp * FNET_GD:(grp + 1) * FNET_GD] = pq[:, FNET_GD:].astype(BF16)


def _fnet_chan(h, mods, g_norm, cs_chan):
    return pl.pallas_call(
        _fnet_chan_kernel,
        out_shape=jax.ShapeDtypeStruct((2, T, D), BF16),
        grid=(T // TM,),
        in_specs=[
            pl.BlockSpec((TM, D), lambda i: (i, 0)),
            _const_spec((1, D)),
            _mod_spec(0, TM), _mod_spec(1, TM),
            _const_spec((FNET_GD, 2 * FNET_GD)),
        ],
        out_specs=pl.BlockSpec((2, TM, D), lambda i: (0, i, 0)),
        compiler_params=_cp(("arbitrary",)),
        name="fnet_channel_dft",
    )(h, g_norm, mods, mods, cs_chan)


def _fnet_seq_kernel(tab_ref, pq_ref, h_ref, gt_ref, w_ref, o_ref, acc_ref, *, nk, inv_norm):
    k = pl.program_id(2)

    @pl.when(k == 0)
    def _():
        acc_ref[...] = jnp.zeros_like(acc_ref)

    acc_ref[...] += _dot(tab_ref[0], pq_ref[0])

    @pl.when(k == nk - 1)
    def _():
        f = (acc_ref[...] * inv_norm).astype(BF16)
        o_ref[...] = h_ref[...] + gt_ref[...] * _dot(f, w_ref[...])


def _fnet_seq(h, pq, table, mods, w, *, seq, row0, mod_row_fn, name):
    tm = min(TM, seq)
    tk = min(1024, seq)
    nkk = seq // tk
    nk = 2 * nkk
    mt = seq // tm
    rb0 = row0 // tm
    kb0 = row0 // tk
    kern = functools.partial(_fnet_seq_kernel, nk=nk, inv_norm=1.0 / np.sqrt(seq * FNET_GD))
    return pl.pallas_call(
        kern,
        out_shape=jax.ShapeDtypeStruct((T, D), F32),
        grid=(B, mt, nk),
        in_specs=[
            pl.BlockSpec((1, tm, tk), lambda b, m, k: (k // nkk, m, k % nkk)),
            pl.BlockSpec((1, tk, D), lambda b, m, k: (k // nkk, kb0 + b * nkk + k % nkk, 0)),
            pl.BlockSpec((tm, D), lambda b, m, k: (rb0 + b * mt + m, 0)),
            pl.BlockSpec((None, 1, D), lambda b, m, k: (mod_row_fn(b) * N_MOD + 2, 0, 0)),
            _const_spec((D, D)),
        ],
        out_specs=pl.BlockSpec((tm, D), lambda b, m, k: (rb0 + b * mt + m, 0)),
        scratch_shapes=[pltpu.VMEM((tm, D), F32)],
        input_output_aliases={2: 0},
        compiler_params=_cp(("arbitrary", "arbitrary", "arbitrary")),
        name=name,
    )(table, pq, h, mods, w)


def _dft_tables(n):
    r = int(np.sqrt(n))
    assert r * r == n
    t = jnp.arange(n, dtype=jnp.int32)[None, :]
    f = jnp.arange(r, dtype=jnp.int32)[:, None]
    a_hi = ((f * t) % r).astype(F32) * (2.0 * np.pi / r)
    a_lo = ((f * t) % n).astype(F32) * (2.0 * np.pi / n)
    c1, s1 = jnp.cos(a_hi)[:, None, :], jnp.sin(a_hi)[:, None, :]
    c0, s0 = jnp.cos(a_lo)[None, :, :], jnp.sin(a_lo)[None, :, :]
    cos = (c1 * c0 - s1 * s0).reshape(n, n)
    sin = (s1 * c0 + c1 * s0).reshape(n, n)
    return cos, sin


def _na_bias_index():
    rows = L // GRID_W
    qi = np.arange(NA_QROWS)[:, None, None, None]
    qc = np.arange(GRID_W)[None, :, None, None]
    kj = np.arange(NA_KROWS)[None, None, :, None]
    kc = np.arange(GRID_W)[None, None, None, :]
    ro_all, co_all, ok_all = [], [], []
    for r0, k0 in ((0, 0), (NA_QROWS, 0), (rows - NA_QROWS, rows - NA_KROWS)):
        r = r0 + qi
        kr = k0 + kj
        rs = np.clip(r - NA_WIN_ROWS // 2, 0, rows - NA_WIN_ROWS)
        cs = np.clip(qc - NA_WIN_COLS // 2, 0, GRID_W - NA_WIN_COLS)
        ok = (kr >= rs) & (kr < rs + NA_WIN_ROWS) & (kc >= cs) & (kc < cs + NA_WIN_COLS)
        ro = np.clip(kr - r + NA_WIN_ROWS - 1, 0, 2 * NA_WIN_ROWS - 2)
        co = np.clip(kc - qc + NA_WIN_COLS - 1, 0, 2 * NA_WIN_COLS - 2)
        shape = (NA_QROWS, GRID_W, NA_KROWS, GRID_W)
        ro_all.append(np.broadcast_to(ro, shape).reshape(NA_TQ, NA_TK))
        co_all.append(np.broadcast_to(co, shape).reshape(NA_TQ, NA_TK))
        ok_all.append(np.broadcast_to(ok, shape).reshape(NA_TQ, NA_TK))
    return np.stack(ro_all), np.stack(co_all), np.stack(ok_all)


def _softmax_pv(s_list, v_list):
    m = None
    for s in s_list:
        sm = jnp.max(s, axis=-1, keepdims=True)
        m = sm if m is None else jnp.maximum(m, sm)
    l = None
    o = None
    for s, v in zip(s_list, v_list):
        p = jnp.exp(s - m)
        ps = jnp.sum(p, axis=-1, keepdims=True)
        pv = _dot(p.astype(BF16), v)
        l = ps if l is None else l + ps
        o = pv if o is None else o + pv
    return o / l


def _na_kernel(q_ref, k_ref, v_ref, kc_ref, vc_ref, bias_ref, o_ref):
    g = pl.program_id(2)
    k0 = jnp.clip(NA_QROWS * g - NA_WIN_ROWS // 2, 0, L // GRID_W - NA_KROWS)
    start = pl.multiple_of(k0 * GRID_W, GRID_W)
    kw = k_ref[pl.ds(start, NA_TK), :]
    vw = v_ref[pl.ds(start, NA_TK), :]
    kc = kc_ref[...]
    vc = vc_ref[...]
    q = q_ref[...]
    lane = lax.broadcasted_iota(jnp.int32, (1, LANES), 1)
    outs = []
    for a in range(2):
        qm = jnp.where((lane // NA_DH) == a, q, jnp.zeros_like(q))
        s_nb = _dot_nt(qm, kw) + bias_ref[0, a]
        s_cx = _dot_nt(qm, kc)
        outs.append(_softmax_pv([s_nb, s_cx], [vw, vc]))
    o_ref[...] = jnp.where(lane < NA_DH, outs[0], outs[1]).astype(BF16)


def _na_attend(qkv, bias):
    hp = D // LANES
    nq = L // NA_TQ

    def cls(g):
        return jnp.where(g == 0, 0, jnp.where(g == nq - 1, 2, 1))

    return pl.pallas_call(
        _na_kernel,
        out_shape=jax.ShapeDtypeStruct((T, D), BF16),
        grid=(B, hp, nq),
        in_specs=[
            pl.BlockSpec((NA_TQ, LANES), lambda b, h, g: (b * nq + g, h)),
            pl.BlockSpec((L, LANES), lambda b, h, g: (b, hp + h)),
            pl.BlockSpec((L, LANES), lambda b, h, g: (b, 2 * hp + h)),
            pl.BlockSpec((NCTX, LANES), lambda b, h, g: (T_LAT // NCTX + b, hp + h)),
            pl.BlockSpec((NCTX, LANES), lambda b, h, g: (T_LAT // NCTX + b, 2 * hp + h)),
            pl.BlockSpec((1, 2, NA_TQ, NA_TK), lambda b, h, g: (cls(g), h, 0, 0)),
        ],
        out_specs=pl.BlockSpec((NA_TQ, LANES), lambda b, h, g: (b * nq + g, h)),
        compiler_params=_cp(("arbitrary", "arbitrary", "arbitrary")),
        name="na_attention",
    )(qkv, qkv, qkv, qkv, qkv, bias)


def _ctx_attn_kernel(q_ref, kc_ref, vc_ref, o_in_ref, o_ref):
    del o_in_ref
    kc = kc_ref[...]
    vc = vc_ref[...]
    q = q_ref[...]
    lane = lax.broadcasted_iota(jnp.int32, (1, LANES), 1)
    outs = []
    for a in range(2):
        qm = jnp.where((lane // NA_DH) == a, q, jnp.zeros_like(q))
        outs.append(_softmax_pv([_dot_nt(qm, kc)], [vc]))
    o_ref[...] = jnp.where(lane < NA_DH, outs[0], outs[1]).astype(BF16)


def _na_ctx_attend(qkv, o):
    hp = D // LANES
    cb = T_LAT // NCTX
    return pl.pallas_call(
        _ctx_attn_kernel,
        out_shape=jax.ShapeDtypeStruct((T, D), BF16),
        grid=(B, hp),
        in_specs=[
            pl.BlockSpec((NCTX, LANES), lambda b, h: (cb + b, h)),
            pl.BlockSpec((NCTX, LANES), lambda b, h: (cb + b, hp + h)),
            pl.BlockSpec((NCTX, LANES), lambda b, h: (cb + b, 2 * hp + h)),
            pl.BlockSpec(memory_space=pl.ANY),
        ],
        out_specs=pl.BlockSpec((NCTX, LANES), lambda b, h: (cb + b, h)),
        input_output_aliases={3: 0},
        compiler_params=_cp(("arbitrary", "arbitrary")),
        name="na_ctx_attention",
    )(qkv, qkv, qkv, o)


def _sg_kernel(h_ref, g_ref, sh_ref, sc_ref, gt_ref, win_ref, gv_ref, ws_ref, bs_ref, wout_ref, o_ref,
               u_ref, v_ref, t_ref):
    h = h_ref[...]
    a = _modulate(h, g_ref[...], sh_ref[...], sc_ref[...]).astype(BF16)
    tn = 512
    ssq = jnp.zeros((SG_TM, 1), F32)
    for c in range(2 * D // tn):
        z = jax.nn.gelu(_dot(a, win_ref[:, c * tn:(c + 1) * tn]), approximate=True)
        if c * tn < D:
            u_ref[:, c * tn:(c + 1) * tn] = z
        else:
            v_ref[:, c * tn - D:(c + 1) * tn - D] = z
            ssq = ssq + jnp.sum(z * z, axis=-1, keepdims=True)
    scale = lax.rsqrt(ssq * (1.0 / D) + EPS)
    vn = (v_ref[...] * scale * gv_ref[...]).astype(BF16)
    for ch in range(SG_TM // SG_CHUNK):
        r0 = ch * SG_CHUNK
        for grp in range(SG_GROUPS):
            c0 = grp * SG_GD
            mixed = _dot(ws_ref[grp], vn[r0:r0 + SG_CHUNK, c0:c0 + SG_GD])
            bs = bs_ref[grp]
            mixed = mixed + jnp.concatenate([bs, bs], axis=-1)
            t_ref[r0:r0 + SG_CHUNK, c0:c0 + SG_GD] = (u_ref[r0:r0 + SG_CHUNK, c0:c0 + SG_GD] * mixed).astype(BF16)
    o_ref[...] = h + gt_ref[...] * _dot(t_ref[...], wout_ref[...])


def _sg(h, mods, g_norm, w_in, g_v, w_s, b_s, w_out):
    return pl.pallas_call(
        _sg_kernel,
        out_shape=jax.ShapeDtypeStruct((T, D), F32),
        grid=(T // SG_TM,),
        in_specs=[
            pl.BlockSpec((SG_TM, D), lambda i: (i, 0)),
            _const_spec((1, D)),
            _mod_spec(0, SG_TM), _mod_spec(1, SG_TM), _mod_spec(2, SG_TM),
            _const_spec((D, 2 * D)),
            _const_spec((1, D)),
            _const_spec((SG_GROUPS, SG_CHUNK, SG_CHUNK)),
            _const_spec((SG_GROUPS, SG_CHUNK, LANES)),
            _const_spec((D, D)),
        ],
        out_specs=pl.BlockSpec((SG_TM, D), lambda i: (i, 0)),
        scratch_shapes=[pltpu.VMEM((SG_TM, D), F32), pltpu.VMEM((SG_TM, D), F32), pltpu.VMEM((SG_TM, D), BF16)],
        compiler_params=_cp(("arbitrary",)),
        name="spatial_gating",
    )(h, g_norm, mods, mods, mods, w_in, g_v, w_s, b_s, w_out)


def _gqa_qkv_kernel(h_ref, g_ref, sh_ref, sc_ref, w_ref, gq_ref, gk_ref, cos_ref, sin_ref, o_ref):
    a = _modulate(h_ref[...], g_ref[...], sh_ref[...], sc_ref[...]).astype(BF16)
    cos = cos_ref[...]
    sin = sin_ref[...]
    lane = lax.broadcasted_iota(jnp.int32, (1, LANES), 1)
    first_half = (lane % ATT_DH) < (ATT_DH // 2)
    ri = lax.broadcasted_iota(jnp.int32, (LANES, LANES), 0) // ATT_DH
    ci = lax.broadcasted_iota(jnp.int32, (LANES, LANES), 1) // ATT_DH
    head_mean = jnp.where(ri == ci, 1.0 / ATT_DH, 0.0).astype(BF16)
    n_qk = (ATT_HEADS + ATT_KV) * ATT_DH
    n_q = ATT_HEADS * ATT_DH
    tn = 256
    for c in range(w_ref.shape[1] // tn):
        y2 = _dot(a, w_ref[:, c * tn:(c + 1) * tn])
        for s in range(tn // LANES):
            col = c * tn + s * LANES
            y = y2[:, s * LANES:(s + 1) * LANES]
            if col < n_qk:
                gain = gq_ref[...] if col < n_q else gk_ref[...]
                ms = _dot((y * y).astype(BF16), head_mean)
                yn = y * lax.rsqrt(ms + EPS) * gain
                partner = jnp.where(first_half, pltpu.roll(yn, LANES - ATT_DH // 2, 1), pltpu.roll(yn, ATT_DH // 2, 1))
                y = yn * cos + partner * sin
            o_ref[:, col:col + LANES] = y.astype(BF16)


def _gqa_qkv(h, mods, g_norm, w, gq, gk, cos_t, sin_t):
    n = w.shape[1]
    tab_spec = pl.BlockSpec((TM, LANES), lambda i: (jnp.where(i * TM >= T_LAT, L // TM, i % (L // TM)), 0))
    return pl.pallas_call(
        _gqa_qkv_kernel,
        out_shape=jax.ShapeDtypeStruct((T, n), BF16),
        grid=(T // TM,),
        in_specs=[
            pl.BlockSpec((TM, D), lambda i: (i, 0)),
            _const_spec((1, D)),
            _mod_spec(0, TM), _mod_spec(1, TM),
            _const_spec((D, n)),
            _const_spec((1, LANES)), _const_spec((1, LANES)),
            tab_spec, tab_spec,
        ],
        out_specs=pl.BlockSpec((TM, n), lambda i: (i, 0)),
        compiler_params=_cp(("arbitrary",)),
        name="gqa_qkv_rope",
    )(h, g_norm, mods, mods, w, gq, gk, cos_t, sin_t)


def _gqa_kernel(q_ref, k_ref, v_ref, kc_ref, vc_ref, o_ref):
    lane = lax.broadcasted_iota(jnp.int32, (1, LANES), 1)
    nq = q_ref.shape[1] // LANES
    res = []
    for a in range(2):
        sel = (lane // ATT_DH) == a
        qs = jnp.concatenate(
            [jnp.where(sel, q_ref[:, i * LANES:(i + 1) * LANES], jnp.zeros((GQ_TQ, LANES), BF16)) for i in range(nq)],
            axis=0)
        m = jnp.full((nq * GQ_TQ, 1), -jnp.inf, F32)
        acc = jnp.zeros((nq * GQ_TQ, LANES), F32)
        chunks = [(k_ref, v_ref, c * GQ_CK, GQ_CK) for c in range(L // GQ_CK)] + [(kc_ref, vc_ref, 0, NCTX)]
        for kr, vr, c0, cn in chunks:
            kk = kr[c0:c0 + cn, :]
            vv = vr[c0:c0 + cn, :]
            va = jnp.where(sel, vv, jnp.ones_like(vv))
            s = _dot_nt(qs, kk)
            m_new = jnp.maximum(m, jnp.max(s, axis=-1, keepdims=True))
            alpha = jnp.exp(m - m_new)
            p = jnp.exp(s - m_new).astype(BF16)
            acc = alpha * acc + _dot(p, va)
            m = m_new
        l = jnp.sum(jnp.where(sel, 0.0, acc), axis=-1, keepdims=True) * (1.0 / ATT_DH)
        res.append(acc / l)
    for i in range(nq):
        o_ref[:, i * LANES:(i + 1) * LANES] = jnp.where(
            lane < ATT_DH, res[0][i * GQ_TQ:(i + 1) * GQ_TQ], res[1][i * GQ_TQ:(i + 1) * GQ_TQ]).astype(BF16)


def _gqa_attend(qkv):
    npair = ATT_KV // 2
    qw = ATT_HEADS * ATT_DH // npair
    nqt = L // GQ_TQ
    kcol = ATT_HEADS * ATT_DH // LANES
    vcol = kcol + npair
    cb = T_LAT // NCTX
    return pl.pallas_call(
        _gqa_kernel,
        out_shape=jax.ShapeDtypeStruct((T, D), BF16),
        grid=(B, npair, nqt),
        in_specs=[
            pl.BlockSpec((GQ_TQ, qw), lambda b, p, t: (b * nqt + t, p)),
            pl.BlockSpec((L, LANES), lambda b, p, t: (b, kcol + p)),
            pl.BlockSpec((L, LANES), lambda b, p, t: (b, vcol + p)),
            pl.BlockSpec((NCTX, LANES), lambda b, p, t: (cb + b, kcol + p)),
            pl.BlockSpec((NCTX, LANES), lambda b, p, t: (cb + b, vcol + p)),
        ],
        out_specs=pl.BlockSpec((GQ_TQ, qw), lambda b, p, t: (b * nqt + t, p)),
        compiler_params=_cp(("arbitrary", "arbitrary", "arbitrary")),
        name="gqa_attention",
    )(qkv, qkv, qkv, qkv, qkv)


def _gqa_head_perm():
    grp = ATT_HEADS // ATT_KV
    order = []
    for t in range(ATT_HEADS // 2):
        p, i = divmod(t, grp)
        order += [2 * grp * p + i, 2 * grp * p + grp + i]
    return np.array(order)


def _rope_tables():
    t = jnp.arange(L)
    row = (t // GRID_W).astype(F32)
    col = (t % GRID_W).astype(F32)
    n_freq = ATT_DH // 4
    inv_freq = ROPE_THETA ** (-jnp.arange(n_freq, dtype=F32) / n_freq)
    ang = jnp.concatenate([row[:, None] * inv_freq, col[:, None] * inv_freq], axis=-1)
    cos, sin = jnp.cos(ang), jnp.sin(ang)
    cos_t = jnp.concatenate([cos, cos, cos, cos], axis=-1)
    sin_t = jnp.concatenate([-sin, sin, -sin, sin], axis=-1)
    cos_t = jnp.concatenate([cos_t, jnp.ones((TM, LANES), F32)], axis=0)
    sin_t = jnp.concatenate([sin_t, jnp.zeros((TM, LANES), F32)], axis=0)
    return cos_t, sin_t


def kernel(x, c, ctx, c_ctx, w_mod, b_mod, g_norm_mix, g_norm_ffn, w_ffn_up, w_ffn_conv, b_ffn_conv, w_ffn_down,
           w_fnet_out, w_na_qkv, na_rel_bias, w_na_out, w_sg_in, g_sg_v, w_sg_spatial, b_sg_spatial, w_sg_out,
           w_att_qkv, g_att_q, g_att_k, w_att_out, g_final):
    assert x.shape == (B, L, D) and ctx.shape == (B, NCTX, D) and w_mod.shape[0] == DEPTH == 4

    h = jnp.concatenate([x.reshape(T_LAT, D), ctx.reshape(T_CTX, D)], axis=0)
    cond8 = jnp.concatenate([c, c_ctx[None, :], jnp.zeros((8 - B - 1, D), F32)], axis=0)
    mods_all = _ada(cond8, w_mod, b_mod)

    def layer_mods(i):
        return mods_all[i, :B + 1].reshape((B + 1) * N_MOD, 1, D)

    def ffn(h, i, *, last):
        wg = w_ffn_up[i][:, :FFN_DIM].reshape(D, FFN_NF, FFN_TF).transpose(1, 0, 2).astype(BF16)
        wv = w_ffn_up[i][:, FFN_DIM:].reshape(D, FFN_NF, FFN_TF).transpose(1, 0, 2).astype(BF16)
        wc = w_ffn_conv[i].reshape(3, FFN_NF, FFN_TF).transpose(1, 0, 2)
        bc = b_ffn_conv[i].reshape(FFN_NF, 1, FFN_TF)
        wd = w_ffn_down[i].reshape(FFN_NF, FFN_TF, D).astype(BF16)
        return _ffn(h, layer_mods(i), g_norm_ffn[i][None, :], wg, wv, wc, bc, wd, g_final[None, :],
                    n_blocks=(T_LAT // TM if last else T // TM), final_norm=last)

    mods = layer_mods(0)
    cos_c, sin_c = _dft_tables(FNET_GD)
    cs_chan = jnp.concatenate([cos_c, sin_c], axis=1).astype(BF16)
    pq = _fnet_chan(h, mods, g_norm_mix[0][None, :], cs_chan)
    cos_l, sin_l = _dft_tables(L)
    tab_lat = jnp.stack([cos_l, -sin_l]).astype(BF16)
    tab_ctx = jnp.stack([cos_c, -sin_c]).astype(BF16)
    w_f = w_fnet_out[0].astype(BF16)
    h = _fnet_seq(h, pq, tab_lat, mods, w_f, seq=L, row0=0, mod_row_fn=lambda b: b, name="fnet_seq_lat")
    h = _fnet_seq(h, pq, tab_ctx, mods, w_f, seq=NCTX, row0=T_LAT, mod_row_fn=lambda b: B, name="fnet_seq_ctx")
    h = ffn(h, 0, last=False)

    mods = layer_mods(1)
    scale = NA_DH ** -0.5
    w_qkv = jnp.concatenate([w_na_qkv[0][:, :D] * scale, w_na_qkv[0][:, D:]], axis=1).astype(BF16)
    qkv = _modmm(h, mods, g_norm_mix[1][None, :], w_qkv, name="na_qkv")
    ro, co, ok = _na_bias_index()
    bias = jnp.where(ok[:, None], na_rel_bias[0][:, ro, co].transpose(1, 0, 2, 3), NEG)
    o = _na_attend(qkv, bias)
    o = _na_ctx_attend(qkv, o)
    h = _outproj(h, o, mods, w_na_out[0].astype(BF16), n_blocks=T // TM, name="na_out")
    h = ffn(h, 1, last=False)

    mods = layer_mods(2)
    b_s = jnp.broadcast_to(b_sg_spatial[0][:, :, None], (SG_GROUPS, SG_CHUNK, LANES))
    h = _sg(h, mods, g_norm_mix[2][None, :], w_sg_in[0].astype(BF16), g_sg_v[0][None, :],
            w_sg_spatial[0].astype(BF16), b_s, w_sg_out[0].astype(BF16))
    h = ffn(h, 2, last=False)

    mods = layer_mods(3)
    perm = _gqa_head_perm()
    nq = ATT_HEADS * ATT_DH
    wq = w_att_qkv[0][:, :nq].reshape(D, ATT_HEADS, ATT_DH)[:, perm].reshape(D, nq)
    w_qkv = jnp.concatenate([wq, w_att_qkv[0][:, nq:]], axis=1).astype(BF16)
    gq = jnp.tile(g_att_q[0] * (ATT_DH ** -0.5), LANES // ATT_DH)[None, :]
    gk = jnp.tile(g_att_k[0], LANES // ATT_DH)[None, :]
    cos_t, sin_t = _rope_tables()
    qkv = _gqa_qkv(h, mods, g_norm_mix[3][None, :], w_qkv, gq, gk, cos_t, sin_t)
    o = _gqa_attend(qkv)
    w_o = w_att_out[0].reshape(ATT_HEADS, ATT_DH, D)[perm].reshape(nq, D).astype(BF16)
    h = _outproj(h, o, mods, w_o, n_blocks=T_LAT // TM, name="gqa_out")
    out = ffn(h, 3, last=True)
    return out.reshape(B, L, D)
```

```python
import functools

import numpy as np
import jax
import jax.numpy as jnp
from jax import lax
from jax.experimental import pallas as pl
from jax.experimental.pallas import tpu as pltpu

F32 = jnp.float32
BF16 = jnp.bfloat16

D = 1024
B = 4
L = 4096
NCTX = 256
DEPTH = 4
GRID_W = 64
EPS = 1e-6
N_MOD = 6
T_LAT = B * L
T_CTX = B * NCTX
T = T_LAT + T_CTX

FNET_GROUPS = 4
FNET_GD = D // FNET_GROUPS

NA_HEADS = 16
NA_DH = D // NA_HEADS
NA_WIN_ROWS = 8
NA_WIN_COLS = 16
NA_QROWS = 4
NA_KROWS = 12
NA_TQ = NA_QROWS * GRID_W
NA_TK = NA_KROWS * GRID_W

SG_CHUNK = 128
SG_GROUPS = 4
SG_GD = D // SG_GROUPS

ATT_HEADS = 16
ATT_KV = 4
ATT_DH = 64
ROPE_THETA = 10000.0
GQ_TQ = 256
GQ_CK = 256

FFN_DIM = 2816
FFN_TF = 256
FFN_NF = FFN_DIM // FFN_TF
HALO = 16

TM = 1024
SG_TM = 512
LANES = 128
NEG = -1e30

VMEM_LIMIT = 56 * 1024 * 1024


def _cp(sem, vmem=VMEM_LIMIT):
    return pltpu.CompilerParams(dimension_semantics=sem, vmem_limit_bytes=vmem)


def _mod_row(i, tm):
    return (i * tm) // L


def _mod_spec(k, tm, b0=0):
    return pl.BlockSpec((None, 1, D), lambda i: (_mod_row(b0 + i, tm) * N_MOD + k, 0, 0))


def _const_spec(shape):
    nd = len(shape)
    return pl.BlockSpec(shape, lambda *_: (0,) * nd)


def _modulate(x, g, shift, scale):
    ms = jnp.mean(x * x, axis=-1, keepdims=True)
    return x * lax.rsqrt(ms + EPS) * (g * (1.0 + scale)) + shift


def _dot(a, b):
    return jnp.dot(a, b, preferred_element_type=F32)


def _dot_nt(a, b):
    return lax.dot_general(a, b, (((1,), (1,)), ((), ())), preferred_element_type=F32)


def _ada_kernel(c_ref, w_ref, b_ref, o_ref):
    c = c_ref[...]
    s = (c * jax.nn.sigmoid(c)).astype(BF16)
    o_ref[0] = _dot(s, w_ref[0].astype(BF16)) + b_ref[0]


def _ada(cond8, w_mod, b_mod):
    tn = 2048
    n = N_MOD * D
    return pl.pallas_call(
        _ada_kernel,
        out_shape=jax.ShapeDtypeStruct((DEPTH, 8, n), F32),
        grid=(DEPTH, n // tn),
        in_specs=[
            pl.BlockSpec((8, D), lambda l, j: (0, 0)),
            pl.BlockSpec((1, D, tn), lambda l, j: (l, 0, j)),
            pl.BlockSpec((1, 1, tn), lambda l, j: (l, 0, j)),
        ],
        out_specs=pl.BlockSpec((1, 8, tn), lambda l, j: (l, 0, j)),
        compiler_params=_cp(("arbitrary", "arbitrary")),
        name="ada_params",
    )(cond8, w_mod, b_mod.reshape(DEPTH, 1, n))


def _ffn_kernel(hp_ref, h_ref, hn_ref, g_ref, sh_ref, sc_ref, gt_ref, wg_ref, wv_ref, wc_ref, bc_ref,
                wd_ref, gf_ref, *rest, tm, blocks_per_seq, final_norm):
    o_ref, a_ref = rest[-2:]
    i = pl.program_id(0)
    g = g_ref[...]
    sh = sh_ref[...]
    sc = sc_ref[...]

    a_ref[HALO:HALO + tm, :] = _modulate(h_ref[...], g, sh, sc).astype(BF16)
    if blocks_per_seq == 1:
        a_ref[0:HALO, :] = jnp.zeros((HALO, D), BF16)
        a_ref[HALO + tm:, :] = jnp.zeros((HALO, D), BF16)
    else:
        seq_first = (i % blocks_per_seq) == 0
        seq_last = (i % blocks_per_seq) == blocks_per_seq - 1
        a_ref[0:HALO, :] = jnp.where(seq_first, 0.0, _modulate(hp_ref[...], g, sh, sc)).astype(BF16)
        a_ref[HALO + tm:, :] = jnp.where(seq_last, 0.0, _modulate(hn_ref[...], g, sh, sc)).astype(BF16)

    for j in range(FFN_NF):
        ge = _dot(a_ref[...], wg_ref[j])
        v = _dot(a_ref[HALO:HALO + tm, :], wv_ref[j])
        wc = wc_ref[j]
        gc = (ge[HALO:HALO + tm] * wc[1:2]
              + ge[HALO - 1:HALO - 1 + tm] * wc[0:1]
              + ge[HALO + 1:HALO + 1 + tm] * wc[2:3]
              + bc_ref[j])
        u = (gc * jax.nn.sigmoid(gc) * v).astype(BF16)
        y = _dot(u, wd_ref[j])
        if j == 0:
            o_ref[...] = y
        else:
            o_ref[...] += y

    out = h_ref[...] + gt_ref[...] * o_ref[...]
    if final_norm:
        ms = jnp.mean(out * out, axis=-1, keepdims=True)
        out = out * lax.rsqrt(ms + EPS) * gf_ref[...]
    o_ref[...] = out


def _ffn(h, mods, g_norm, wg, wv, wc, bc, wd, g_final, *, tm, row0, n_blocks, out_rows, prev_out, final_norm):
    hb = tm // HALO
    b0 = row0 // tm
    last_halo = T // HALO - 1
    kern = functools.partial(_ffn_kernel, tm=tm, blocks_per_seq=max(1, L // tm) if row0 == 0 else NCTX // tm,
                             final_norm=final_norm)
    in_specs = [
        pl.BlockSpec((HALO, D), lambda i: (jnp.maximum((b0 + i) * hb - 1, 0), 0)),
        pl.BlockSpec((tm, D), lambda i: (b0 + i, 0)),
        pl.BlockSpec((HALO, D), lambda i: (jnp.minimum((b0 + i + 1) * hb, last_halo), 0)),
        _const_spec((1, D)),
        _mod_spec(3, tm, b0), _mod_spec(4, tm, b0), _mod_spec(5, tm, b0),
        _const_spec((FFN_NF, D, FFN_TF)),
        _const_spec((FFN_NF, D, FFN_TF)),
        _const_spec((FFN_NF, 3, FFN_TF)),
        _const_spec((FFN_NF, 1, FFN_TF)),
        _const_spec((FFN_NF, FFN_TF, D)),
        _const_spec((1, D)),
    ]
    args = [h, h, h, g_norm, mods, mods, mods, wg, wv, wc, bc, wd, g_final]
    aliases = {}
    if prev_out is not None:
        in_specs.append(pl.BlockSpec(memory_space=pl.ANY))
        args.append(prev_out)
        aliases = {len(args) - 1: 0}
    return pl.pallas_call(
        kern,
        out_shape=jax.ShapeDtypeStruct((out_rows, D), F32),
        grid=(n_blocks,),
        in_specs=in_specs,
        out_specs=pl.BlockSpec((tm, D), lambda i: (b0 + i, 0)),
        scratch_shapes=[pltpu.VMEM((tm + 2 * HALO, D), BF16)],
        input_output_aliases=aliases,
        compiler_params=_cp(("arbitrary",)),
        name="conv_ffn_lat" if row0 == 0 else "conv_ffn_ctx",
    )(*args)


def _modmm_kernel(h_ref, g_ref, sh_ref, sc_ref, w_ref, o_ref, *, tn):
    a = _modulate(h_ref[...], g_ref[...], sh_ref[...], sc_ref[...]).astype(BF16)
    for c in range(w_ref.shape[1] // tn):
        o_ref[:, c * tn:(c + 1) * tn] = _dot(a, w_ref[:, c * tn:(c + 1) * tn]).astype(BF16)


def _modmm(h, mods, g_norm, w, *, name):
    n = w.shape[1]
    return pl.pallas_call(
        functools.partial(_modmm_kernel, tn=512),
        out_shape=jax.ShapeDtypeStruct((T, n), BF16),
        grid=(T // TM,),
        in_specs=[
            pl.BlockSpec((TM, D), lambda i: (i, 0)),
            _const_spec((1, D)),
            _mod_spec(0, TM), _mod_spec(1, TM),
            _const_spec((D, n)),
        ],
        out_specs=pl.BlockSpec((TM, n), lambda i: (i, 0)),
        compiler_params=_cp(("arbitrary",)),
        name=name,
    )(h, g_norm, mods, mods, w)


def _outproj_kernel(h_ref, x_ref, gt_ref, w_ref, o_ref):
    o_ref[...] = h_ref[...] + gt_ref[...] * _dot(x_ref[...], w_ref[...])


def _outproj(h, x, mods, w, *, n_blocks, name):
    return pl.pallas_call(
        _outproj_kernel,
        out_shape=jax.ShapeDtypeStruct((T, D), F32),
        grid=(n_blocks,),
        in_specs=[
            pl.BlockSpec((TM, D), lambda i: (i, 0)),
            pl.BlockSpec((TM, D), lambda i: (i, 0)),
            _mod_spec(2, TM),
            _const_spec((D, D)),
        ],
        out_specs=pl.BlockSpec((TM, D), lambda i: (i, 0)),
        input_output_aliases={0: 0},
        compiler_params=_cp(("arbitrary",)),
        name=name,
    )(h, x, mods, w)


def _fnet_chan_kernel(h_ref, g_ref, sh_ref, sc_ref, cs_ref, o_ref):
    a = _modulate(h_ref[...], g_ref[...], sh_ref[...], sc_ref[...]).astype(BF16)
    cs = cs_ref[...]
    for grp in range(FNET_GROUPS):
        pq = _dot(a[:, grp * FNET_GD:(grp + 1) * FNET_GD], cs)
        o_ref[0, :, grp * FNET_GD:(grp + 1) * FNET_GD] = pq[:, :FNET_GD].astype(BF16)
        o_ref[1, :, grp * FNET_GD:(grp + 1) * FNET_GD] = pq[:, FNET_GD:].astype(BF16)


def _fnet_chan(h, mods, g_norm, cs_chan):
    return pl.pallas_call(
        _fnet_chan_kernel,
        out_shape=jax.ShapeDtypeStruct((2, T, D), BF16),
        grid=(T // TM,),
        in_specs=[
            pl.BlockSpec((TM, D), lambda i: (i, 0)),
            _const_spec((1, D)),
            _mod_spec(0, TM), _mod_spec(1, TM),
            _const_spec((FNET_GD, 2 * FNET_GD)),
        ],
        out_specs=pl.BlockSpec((2, TM, D), lambda i: (0, i, 0)),
        compiler_params=_cp(("arbitrary",)),
        name="fnet_channel_dft",
    )(h, g_norm, mods, mods, cs_chan)


def _fnet_seq_kernel(tab_ref, pq_ref, h_ref, gt_ref, w_ref, o_ref, acc_ref, *, nk, inv_norm):
    k = pl.program_id(2)

    @pl.when(k == 0)
    def _():
        acc_ref[...] = jnp.zeros_like(acc_ref)

    acc_ref[...] += _dot(tab_ref[0], pq_ref[0])

    @pl.when(k == nk - 1)
    def _():
        f = (acc_ref[...] * inv_norm).astype(BF16)
        o_ref[...] = h_ref[...] + gt_ref[...] * _dot(f, w_ref[...])


def _fnet_seq(h, pq, table, mods, w, *, seq, row0, mod_row_fn, name):
    tm = min(TM, seq)
    tk = min(1024, seq)
    nkk = seq // tk
    nk = 2 * nkk
    mt = seq // tm
    rb0 = row0 // tm
    kb0 = row0 // tk
    kern = functools.partial(_fnet_seq_kernel, nk=nk, inv_norm=1.0 / np.sqrt(seq * FNET_GD))
    return pl.pallas_call(
        kern,
        out_shape=jax.ShapeDtypeStruct((T, D), F32),
        grid=(B, mt, nk),
        in_specs=[
            pl.BlockSpec((1, tm, tk), lambda b, m, k: (k // nkk, m, k % nkk)),
            pl.BlockSpec((1, tk, D), lambda b, m, k: (k // nkk, kb0 + b * nkk + k % nkk, 0)),
            pl.BlockSpec((tm, D), lambda b, m, k: (rb0 + b * mt + m, 0)),
            pl.BlockSpec((None, 1, D), lambda b, m, k: (mod_row_fn(b) * N_MOD + 2, 0, 0)),
            _const_spec((D, D)),
        ],
        out_specs=pl.BlockSpec((tm, D), lambda b, m, k: (rb0 + b * mt + m, 0)),
        scratch_shapes=[pltpu.VMEM((tm, D), F32)],
        input_output_aliases={2: 0},
        compiler_params=_cp(("arbitrary", "arbitrary", "arbitrary")),
        name=name,
    )(table, pq, h, mods, w)


def _dft_tables(n):
    r = int(np.sqrt(n))
    assert r * r == n
    t = jnp.arange(n, dtype=jnp.int32)[None, :]
    f = jnp.arange(r, dtype=jnp.int32)[:, None]
    a_hi = ((f * t) % r).astype(F32) * (2.0 * np.pi / r)
    a_lo = ((f * t) % n).astype(F32) * (2.0 * np.pi / n)
    c1, s1 = jnp.cos(a_hi)[:, None, :], jnp.sin(a_hi)[:, None, :]
    c0, s0 = jnp.cos(a_lo)[None, :, :], jnp.sin(a_lo)[None, :, :]
    cos = (c1 * c0 - s1 * s0).reshape(n, n)
    sin = (s1 * c0 + c1 * s0).reshape(n, n)
    return cos, sin


def _na_bias_tiles(rpb):
    rows = L // GRID_W
    n_ro, n_co = 2 * NA_WIN_ROWS - 1, 2 * NA_WIN_COLS - 1
    qc = np.arange(GRID_W)[:, None]
    kc = np.arange(GRID_W)[None, :]
    cs = np.clip(qc - NA_WIN_COLS // 2, 0, GRID_W - NA_WIN_COLS)
    col_ok = (kc >= cs) & (kc < cs + NA_WIN_COLS)
    col_sel = (kc - qc + NA_WIN_COLS - 1)[None] == np.arange(n_co)[:, None, None]
    qi = np.arange(NA_QROWS)[:, None]
    kj = np.arange(NA_KROWS)[None, :]
    row_sel, row_ok = [], []
    for r0, k0 in ((0, 0), (NA_QROWS, 0), (rows - NA_QROWS, rows - NA_KROWS)):
        r = r0 + qi
        kr = k0 + kj
        rs = np.clip(r - NA_WIN_ROWS // 2, 0, rows - NA_WIN_ROWS)
        row_ok.append((kr >= rs) & (kr < rs + NA_WIN_ROWS))
        row_sel.append((kr - r + NA_WIN_ROWS - 1)[:, :, None] == np.arange(n_ro))
    row_sel = np.stack(row_sel).astype(np.float32)
    ok = np.stack(row_ok)[:, :, None, :, None] & col_ok[None, None, :, None, :]
    hi = lax.Precision.HIGHEST
    by_col = jnp.einsum("hrd,dqk->hrqk", rpb, jnp.asarray(col_sel, F32), precision=hi)
    bias = jnp.einsum("cijr,hrqk->chiqjk", jnp.asarray(row_sel), by_col, precision=hi)
    bias = jnp.where(ok[:, None], bias, NEG)
    return bias.reshape(3, NA_HEADS, NA_TQ, NA_TK)


def _softmax_pv(s_list, v_list):
    m = None
    for s in s_list:
        sm = jnp.max(s, axis=-1, keepdims=True)
        m = sm if m is None else jnp.maximum(m, sm)
    l = None
    o = None
    for s, v in zip(s_list, v_list):
        p = jnp.exp(s - m)
        ps = jnp.sum(p, axis=-1, keepdims=True)
        pv = _dot(p.astype(BF16), v)
        l = ps if l is None else l + ps
        o = pv if o is None else o + pv
    return o / l


def _na_kernel(q_ref, k_ref, v_ref, kc_ref, vc_ref, bias_ref, o_ref):
    g = pl.program_id(2)
    k0 = jnp.clip(NA_QROWS * g - NA_WIN_ROWS // 2, 0, L // GRID_W - NA_KROWS)
    start = pl.multiple_of(k0 * GRID_W, GRID_W)
    kw = k_ref[pl.ds(start, NA_TK), :]
    vw = v_ref[pl.ds(start, NA_TK), :]
    kc = kc_ref[...]
    vc = vc_ref[...]
    q = q_ref[...]
    lane = lax.broadcasted_iota(jnp.int32, (1, LANES), 1)
    outs = []
    for a in range(2):
        qm = jnp.where((lane // NA_DH) == a, q, jnp.zeros_like(q))
        s_nb = _dot_nt(qm, kw) + bias_ref[0, a]
        s_cx = _dot_nt(qm, kc)
        outs.append(_softmax_pv([s_nb, s_cx], [vw, vc]))
    o_ref[...] = jnp.where(lane < NA_DH, outs[0], outs[1]).astype(BF16)


def _na_attend(qkv, bias):
    hp = D // LANES
    nq = L // NA_TQ

    def cls(g):
        return jnp.where(g == 0, 0, jnp.where(g == nq - 1, 2, 1))

    return pl.pallas_call(
        _na_kernel,
        out_shape=jax.ShapeDtypeStruct((T, D), BF16),
        grid=(B, hp, nq),
        in_specs=[
            pl.BlockSpec((NA_TQ, LANES), lambda b, h, g: (b * nq + g, h)),
            pl.BlockSpec((L, LANES), lambda b, h, g: (b, hp + h)),
            pl.BlockSpec((L, LANES), lambda b, h, g: (b, 2 * hp + h)),
            pl.BlockSpec((NCTX, LANES), lambda b, h, g: (T_LAT // NCTX + b, hp + h)),
            pl.BlockSpec((NCTX, LANES), lambda b, h, g: (T_LAT // NCTX + b, 2 * hp + h)),
            pl.BlockSpec((1, 2, NA_TQ, NA_TK), lambda b, h, g: (cls(g), h, 0, 0)),
        ],
        out_specs=pl.BlockSpec((NA_TQ, LANES), lambda b, h, g: (b * nq + g, h)),
        compiler_params=_cp(("arbitrary", "arbitrary", "arbitrary")),
        name="na_attention",
    )(qkv, qkv, qkv, qkv, qkv, bias)


def _ctx_attn_kernel(q_ref, kc_ref, vc_ref, o_in_ref, o_ref):
    del o_in_ref
    kc = kc_ref[...]
    vc = vc_ref[...]
    q = q_ref[...]
    lane = lax.broadcasted_iota(jnp.int32, (1, LANES), 1)
    outs = []
    for a in range(2):
        qm = jnp.where((lane // NA_DH) == a, q, jnp.zeros_like(q))
        outs.append(_softmax_pv([_dot_nt(qm, kc)], [vc]))
    o_ref[...] = jnp.where(lane < NA_DH, outs[0], outs[1]).astype(BF16)


def _na_ctx_attend(qkv, o):
    hp = D // LANES
    cb = T_LAT // NCTX
    return pl.pallas_call(
        _ctx_attn_kernel,
        out_shape=jax.ShapeDtypeStruct((T, D), BF16),
        grid=(B, hp),
        in_specs=[
            pl.BlockSpec((NCTX, LANES), lambda b, h: (cb + b, h)),
            pl.BlockSpec((NCTX, LANES), lambda b, h: (cb + b, hp + h)),
            pl.BlockSpec((NCTX, LANES), lambda b, h: (cb + b, 2 * hp + h)),
            pl.BlockSpec(memory_space=pl.ANY),
        ],
        out_specs=pl.BlockSpec((NCTX, LANES), lambda b, h: (cb + b, h)),
        input_output_aliases={3: 0},
        compiler_params=_cp(("arbitrary", "arbitrary")),
        name="na_ctx_attention",
    )(qkv, qkv, qkv, o)


def _sg_kernel(h_ref, g_ref, sh_ref, sc_ref, gt_ref, win_ref, gv_ref, ws_ref, bs_ref, wout_ref, o_ref,
               u_ref, v_ref, t_ref):
    h = h_ref[...]
    a = _modulate(h, g_ref[...], sh_ref[...], sc_ref[...]).astype(BF16)
    tn = 512
    ssq = jnp.zeros((SG_TM, 1), F32)
    for c in range(2 * D // tn):
        z = jax.nn.gelu(_dot(a, win_ref[:, c * tn:(c + 1) * tn]), approximate=True)
        if c * tn < D:
            u_ref[:, c * tn:(c + 1) * tn] = z
        else:
            v_ref[:, c * tn - D:(c + 1) * tn - D] = z
            ssq = ssq + jnp.sum(z * z, axis=-1, keepdims=True)
    scale = lax.rsqrt(ssq * (1.0 / D) + EPS)
    vn = (v_ref[...] * scale * gv_ref[...]).astype(BF16)
    for ch in range(SG_TM // SG_CHUNK):
        r0 = ch * SG_CHUNK
        for grp in range(SG_GROUPS):
            c0 = grp * SG_GD
            mixed = _dot(ws_ref[grp], vn[r0:r0 + SG_CHUNK, c0:c0 + SG_GD])
            bs = bs_ref[grp]
            mixed = mixed + jnp.concatenate([bs, bs], axis=-1)
            t_ref[r0:r0 + SG_CHUNK, c0:c0 + SG_GD] = (u_ref[r0:r0 + SG_CHUNK, c0:c0 + SG_GD] * mixed).astype(BF16)
    o_ref[...] = h + gt_ref[...] * _dot(t_ref[...], wout_ref[...])


def _sg(h, mods, g_norm, w_in, g_v, w_s, b_s, w_out):
    return pl.pallas_call(
        _sg_kernel,
        out_shape=jax.ShapeDtypeStruct((T, D), F32),
        grid=(T // SG_TM,),
        in_specs=[
            pl.BlockSpec((SG_TM, D), lambda i: (i, 0)),
            _const_spec((1, D)),
            _mod_spec(0, SG_TM), _mod_spec(1, SG_TM), _mod_spec(2, SG_TM),
            _const_spec((D, 2 * D)),
            _const_spec((1, D)),
            _const_spec((SG_GROUPS, SG_CHUNK, SG_CHUNK)),
            _const_spec((SG_GROUPS, SG_CHUNK, LANES)),
            _const_spec((D, D)),
        ],
        out_specs=pl.BlockSpec((SG_TM, D), lambda i: (i, 0)),
        scratch_shapes=[pltpu.VMEM((SG_TM, D), F32), pltpu.VMEM((SG_TM, D), F32), pltpu.VMEM((SG_TM, D), BF16)],
        compiler_params=_cp(("arbitrary",)),
        name="spatial_gating",
    )(h, g_norm, mods, mods, mods, w_in, g_v, w_s, b_s, w_out)


def _gqa_qkv_kernel(h_ref, g_ref, sh_ref, sc_ref, w_ref, gq_ref, gk_ref, cos_ref, sin_ref,
                    q_ref, k_ref, v_ref):
    a = _modulate(h_ref[...], g_ref[...], sh_ref[...], sc_ref[...]).astype(BF16)
    cos = cos_ref[...]
    sin = sin_ref[...]
    lane = lax.broadcasted_iota(jnp.int32, (1, LANES), 1)
    head_lanes = lane < ATT_DH
    first_half = (lane % ATT_DH) < (ATT_DH // 2)
    ri = lax.broadcasted_iota(jnp.int32, (LANES, LANES), 0) // ATT_DH
    ci = lax.broadcasted_iota(jnp.int32, (LANES, LANES), 1) // ATT_DH
    head_mean = jnp.where(ri == ci, 1.0 / ATT_DH, 0.0).astype(BF16)
    n_qk = (ATT_HEADS + ATT_KV) * ATT_DH
    n_q = ATT_HEADS * ATT_DH
    tn = 256
    for c in range(w_ref.shape[1] // tn):
        y2 = _dot(a, w_ref[:, c * tn:(c + 1) * tn])
        for s in range(tn // LANES):
            col = c * tn + s * LANES
            y = y2[:, s * LANES:(s + 1) * LANES]
            if col < n_qk:
                gain = gq_ref[...] if col < n_q else gk_ref[...]
                ms = _dot((y * y).astype(BF16), head_mean)
                yn = y * lax.rsqrt(ms + EPS) * gain
                partner = jnp.where(first_half, pltpu.roll(yn, LANES - ATT_DH // 2, 1), pltpu.roll(yn, ATT_DH // 2, 1))
                y = yn * cos + partner * sin
            if col < n_q:
                dst, base, fill = q_ref, col, 0.0
            elif col < n_qk:
                dst, base, fill = k_ref, col - n_q, 0.0
            else:
                dst, base, fill = v_ref, col - n_qk, 1.0
            dst[:, 2 * base:2 * base + LANES] = jnp.where(head_lanes, y, fill).astype(BF16)
            dst[:, 2 * base + LANES:2 * base + 2 * LANES] = jnp.where(
                head_lanes, pltpu.roll(y, ATT_DH, 1), fill).astype(BF16)


def _gqa_qkv(h, mods, g_norm, w, gq, gk, cos_t, sin_t):
    tab_spec = pl.BlockSpec((TM, LANES), lambda i: (jnp.where(i * TM >= T_LAT, L // TM, i % (L // TM)), 0))
    nq, nkv = ATT_HEADS * LANES, ATT_KV * LANES
    return pl.pallas_call(
        _gqa_qkv_kernel,
        out_shape=(jax.ShapeDtypeStruct((T, nq), BF16), jax.ShapeDtypeStruct((T, nkv), BF16),
                   jax.ShapeDtypeStruct((T, nkv), BF16)),
        grid=(T // TM,),
        in_specs=[
            pl.BlockSpec((TM, D), lambda i: (i, 0)),
            _const_spec((1, D)),
            _mod_spec(0, TM), _mod_spec(1, TM),
            _const_spec(w.shape),
            _const_spec((1, LANES)), _const_spec((1, LANES)),
            tab_spec, tab_spec,
        ],
        out_specs=(pl.BlockSpec((TM, nq), lambda i: (i, 0)), pl.BlockSpec((TM, nkv), lambda i: (i, 0)),
                   pl.BlockSpec((TM, nkv), lambda i: (i, 0))),
        compiler_params=_cp(("arbitrary",)),
        name="gqa_qkv_rope",
    )(h, g_norm, mods, mods, w, gq, gk, cos_t, sin_t)


def _gqa_kernel(q_ref, k_ref, v_ref, kc_ref, vc_ref, o_ref):
    lane = lax.broadcasted_iota(jnp.int32, (1, LANES), 1)
    nh = ATT_HEADS // ATT_KV
    qs = jnp.concatenate([q_ref[:, i * LANES:(i + 1) * LANES] for i in range(nh)], axis=0)
    chunks = [(k_ref, v_ref, c * GQ_CK, GQ_CK) for c in range(L // GQ_CK)] + [(kc_ref, vc_ref, 0, NCTX)]
    m = jnp.full((nh * GQ_TQ, 1), -jnp.inf, F32)
    acc = jnp.zeros((nh * GQ_TQ, LANES), F32)
    for kr, vr, c0, cn in chunks:
        s = _dot_nt(qs, kr[c0:c0 + cn, :])
        m_new = jnp.maximum(m, jnp.max(s, axis=-1, keepdims=True))
        p = jnp.exp2(s - m_new).astype(BF16)
        acc = jnp.exp2(m - m_new) * acc + _dot(p, vr[c0:c0 + cn, :])
        m = m_new
    o = acc * (1.0 / acc[:, ATT_DH:ATT_DH + 1])
    for j in range(nh // 2):
        oa = o[2 * j * GQ_TQ:(2 * j + 1) * GQ_TQ]
        ob = o[(2 * j + 1) * GQ_TQ:(2 * j + 2) * GQ_TQ]
        o_ref[:, j * LANES:(j + 1) * LANES] = jnp.where(lane < ATT_DH, oa, pltpu.roll(ob, ATT_DH, 1)).astype(BF16)


def _gqa_attend(q, k, v):
    nh = ATT_HEADS // ATT_KV
    nqt = L // GQ_TQ
    cb = T_LAT // NCTX
    return pl.pallas_call(
        _gqa_kernel,
        out_shape=jax.ShapeDtypeStruct((T, D), BF16),
        grid=(B, ATT_KV, nqt),
        in_specs=[
            pl.BlockSpec((GQ_TQ, nh * LANES), lambda b, h, t: (b * nqt + t, h)),
            pl.BlockSpec((L, LANES), lambda b, h, t: (b, h)),
            pl.BlockSpec((L, LANES), lambda b, h, t: (b, h)),
            pl.BlockSpec((NCTX, LANES), lambda b, h, t: (cb + b, h)),
            pl.BlockSpec((NCTX, LANES), lambda b, h, t: (cb + b, h)),
        ],
        out_specs=pl.BlockSpec((GQ_TQ, nh * ATT_DH), lambda b, h, t: (b * nqt + t, h)),
        compiler_params=_cp(("arbitrary", "arbitrary", "arbitrary")),
        name="gqa_attention",
    )(q, k, v, k, v)


def _rope_tables():
    t = jnp.arange(L)
    row = (t // GRID_W).astype(F32)
    col = (t % GRID_W).astype(F32)
    n_freq = ATT_DH // 4
    inv_freq = ROPE_THETA ** (-jnp.arange(n_freq, dtype=F32) / n_freq)
    ang = jnp.concatenate([row[:, None] * inv_freq, col[:, None] * inv_freq], axis=-1)
    cos, sin = jnp.cos(ang), jnp.sin(ang)
    cos_t = jnp.concatenate([cos, cos, cos, cos], axis=-1)
    sin_t = jnp.concatenate([-sin, sin, -sin, sin], axis=-1)
    cos_t = jnp.concatenate([cos_t, jnp.ones((TM, LANES), F32)], axis=0)
    sin_t = jnp.concatenate([sin_t, jnp.zeros((TM, LANES), F32)], axis=0)
    return cos_t, sin_t


def kernel(x, c, ctx, c_ctx, w_mod, b_mod, g_norm_mix, g_norm_ffn, w_ffn_up, w_ffn_conv, b_ffn_conv, w_ffn_down,
           w_fnet_out, w_na_qkv, na_rel_bias, w_na_out, w_sg_in, g_sg_v, w_sg_spatial, b_sg_spatial, w_sg_out,
           w_att_qkv, g_att_q, g_att_k, w_att_out, g_final):
    assert x.shape == (B, L, D) and ctx.shape == (B, NCTX, D) and w_mod.shape[0] == DEPTH == 4

    h = jnp.concatenate([x.reshape(T_LAT, D), ctx.reshape(T_CTX, D)], axis=0)
    cond8 = jnp.concatenate([c, c_ctx[None, :], jnp.zeros((8 - B - 1, D), F32)], axis=0)
    mods_all = _ada(cond8, w_mod, b_mod)

    def layer_mods(i):
        return mods_all[i, :B + 1].reshape((B + 1) * N_MOD, 1, D)

    def ffn(h, i, *, last):
        wg = w_ffn_up[i][:, :FFN_DIM].reshape(D, FFN_NF, FFN_TF).transpose(1, 0, 2).astype(BF16)
        wv = w_ffn_up[i][:, FFN_DIM:].reshape(D, FFN_NF, FFN_TF).transpose(1, 0, 2).astype(BF16)
        wc = w_ffn_conv[i].reshape(3, FFN_NF, FFN_TF).transpose(1, 0, 2)
        bc = b_ffn_conv[i].reshape(FFN_NF, 1, FFN_TF)
        wd = w_ffn_down[i].reshape(FFN_NF, FFN_TF, D).astype(BF16)
        args = (h, layer_mods(i), g_norm_ffn[i][None, :], wg, wv, wc, bc, wd, g_final[None, :])
        if last:
            return _ffn(*args, tm=TM, row0=0, n_blocks=T_LAT // TM, out_rows=T_LAT, prev_out=None, final_norm=True)
        out = _ffn(*args, tm=TM, row0=0, n_blocks=T_LAT // TM, out_rows=T, prev_out=None, final_norm=False)
        return _ffn(*args, tm=NCTX, row0=T_LAT, n_blocks=B, out_rows=T, prev_out=out, final_norm=False)

    mods = layer_mods(0)
    cos_c, sin_c = _dft_tables(FNET_GD)
    cs_chan = jnp.concatenate([cos_c, sin_c], axis=1).astype(BF16)
    pq = _fnet_chan(h, mods, g_norm_mix[0][None, :], cs_chan)
    cos_l, sin_l = _dft_tables(L)
    tab_lat = jnp.stack([cos_l, -sin_l]).astype(BF16)
    tab_ctx = jnp.stack([cos_c, -sin_c]).astype(BF16)
    w_f = w_fnet_out[0].astype(BF16)
    h = _fnet_seq(h, pq, tab_lat, mods, w_f, seq=L, row0=0, mod_row_fn=lambda b: b, name="fnet_seq_lat")
    h = _fnet_seq(h, pq, tab_ctx, mods, w_f, seq=NCTX, row0=T_LAT, mod_row_fn=lambda b: B, name="fnet_seq_ctx")
    h = ffn(h, 0, last=False)

    mods = layer_mods(1)
    scale = NA_DH ** -0.5
    w_qkv = jnp.concatenate([w_na_qkv[0][:, :D] * scale, w_na_qkv[0][:, D:]], axis=1).astype(BF16)
    qkv = _modmm(h, mods, g_norm_mix[1][None, :], w_qkv, name="na_qkv")
    o = _na_attend(qkv, _na_bias_tiles(na_rel_bias[0]))
    o = _na_ctx_attend(qkv, o)
    h = _outproj(h, o, mods, w_na_out[0].astype(BF16), n_blocks=T // TM, name="na_out")
    h = ffn(h, 1, last=False)

    mods = layer_mods(2)
    b_s = jnp.broadcast_to(b_sg_spatial[0][:, :, None], (SG_GROUPS, SG_CHUNK, LANES))
    h = _sg(h, mods, g_norm_mix[2][None, :], w_sg_in[0].astype(BF16), g_sg_v[0][None, :],
            w_sg_spatial[0].astype(BF16), b_s, w_sg_out[0].astype(BF16))
    h = ffn(h, 2, last=False)

    mods = layer_mods(3)
    gq = jnp.tile(g_att_q[0] * (ATT_DH ** -0.5 * np.log2(np.e)), LANES // ATT_DH)[None, :]
    gk = jnp.tile(g_att_k[0], LANES // ATT_DH)[None, :]
    cos_t, sin_t = _rope_tables()
    q, k, v = _gqa_qkv(h, mods, g_norm_mix[3][None, :], w_att_qkv[0].astype(BF16), gq, gk, cos_t, sin_t)
    o = _gqa_attend(q, k, v)
    h = _outproj(h, o, mods, w_att_out[0].astype(BF16), n_blocks=T_LAT // TM, name="gqa_out")
    out = ffn(h, 3, last=True)
    return out.reshape(B, L, D)
```

```python
import functools

import numpy as np
import jax
import jax.numpy as jnp
from jax import lax
from jax.experimental import pallas as pl
from jax.experimental.pallas import tpu as pltpu

F32 = jnp.float32
BF16 = jnp.bfloat16

D = 1024
B = 4
L = 4096
NCTX = 256
DEPTH = 4
GRID_W = 64
EPS = 1e-6
N_MOD = 6
T_LAT = B * L
T_CTX = B * NCTX
T = T_LAT + T_CTX

FNET_GROUPS = 4
FNET_GD = D // FNET_GROUPS

NA_HEADS = 16
NA_DH = D // NA_HEADS
NA_WIN_ROWS = 8
NA_WIN_COLS = 16
NA_QROWS = 4
NA_KROWS = 12
NA_TQ = NA_QROWS * GRID_W
NA_TK = NA_KROWS * GRID_W
NA_HB = 4

SG_CHUNK = 128
SG_GROUPS = 4
SG_GD = D // SG_GROUPS

ATT_HEADS = 16
ATT_KV = 4
ATT_DH = 64
ROPE_THETA = 10000.0
GQ_TQ = 256
GQ_CK = 256

FFN_DIM = 2816
FFN_TF = 256
FFN_NF = FFN_DIM // FFN_TF
HALO = 16

TM = 1024
SG_TM = 512
LANES = 128
NEG = -1e30
LOG2E = float(np.log2(np.e))

VMEM_LIMIT = 56 * 1024 * 1024


def _cp(sem, vmem=VMEM_LIMIT):
    return pltpu.CompilerParams(dimension_semantics=sem, vmem_limit_bytes=vmem)


def _mod_row(i, tm):
    return (i * tm) // L


def _mod_spec(k, tm, b0=0):
    return pl.BlockSpec((None, 1, D), lambda i: (_mod_row(b0 + i, tm) * N_MOD + k, 0, 0))


def _const_spec(shape, single_buffer=False):
    nd = len(shape)
    if single_buffer:
        return pl.BlockSpec(shape, lambda *_: (0,) * nd, pipeline_mode=pl.Buffered(1))
    return pl.BlockSpec(shape, lambda *_: (0,) * nd)


def _modulate(x, g, shift, scale):
    ms = jnp.mean(x * x, axis=-1, keepdims=True)
    return x * lax.rsqrt(ms + EPS) * (g * (1.0 + scale)) + shift


def _dot(a, b):
    return jnp.dot(a, b, preferred_element_type=F32)


def _dot_nt(a, b):
    return lax.dot_general(a, b, (((1,), (1,)), ((), ())), preferred_element_type=F32)


def _ada_kernel(c_ref, w_ref, b_ref, o_ref):
    c = c_ref[...]
    s = (c * jax.nn.sigmoid(c)).astype(BF16)
    o_ref[0] = _dot(s, w_ref[0].astype(BF16)) + b_ref[0]


def _ada(cond8, w_mod, b_mod):
    tn = 2048
    n = N_MOD * D
    return pl.pallas_call(
        _ada_kernel,
        out_shape=jax.ShapeDtypeStruct((DEPTH, 8, n), F32),
        grid=(DEPTH, n // tn),
        in_specs=[
            pl.BlockSpec((8, D), lambda l, j: (0, 0)),
            pl.BlockSpec((1, D, tn), lambda l, j: (l, 0, j)),
            pl.BlockSpec((1, 1, tn), lambda l, j: (l, 0, j)),
        ],
        out_specs=pl.BlockSpec((1, 8, tn), lambda l, j: (l, 0, j)),
        compiler_params=_cp(("arbitrary", "arbitrary")),
        name="ada_params",
    )(cond8, w_mod, b_mod.reshape(DEPTH, 1, n))


def _ffn_kernel(hp_ref, h_ref, hn_ref, g_ref, sh_ref, sc_ref, gt_ref, wu_ref, wc_ref, bc_ref,
                wd_ref, gf_ref, o_ref, a_ref, gb_ref, vb_ref, *, tm, blocks_per_seq, n_compute, final_norm):
    i = pl.program_id(0)

    @pl.when(i >= n_compute)
    def _():
        o_ref[...] = h_ref[...]

    @pl.when(i < n_compute)
    def _():
        _ffn_block(hp_ref, h_ref, hn_ref, g_ref, sh_ref, sc_ref, gt_ref, wu_ref, wc_ref, bc_ref, wd_ref,
                   gf_ref, o_ref, a_ref, gb_ref, vb_ref, tm=tm, blocks_per_seq=blocks_per_seq, final_norm=final_norm)


def _ffn_block(hp_ref, h_ref, hn_ref, g_ref, sh_ref, sc_ref, gt_ref, wu_ref, wc_ref, bc_ref, wd_ref,
               gf_ref, o_ref, a_ref, gb_ref, vb_ref, *, tm, blocks_per_seq, final_norm):
    i = pl.program_id(0)
    g = g_ref[...]
    sh = sh_ref[...]
    sc = sc_ref[...]

    a_ref[HALO:HALO + tm, :] = _modulate(h_ref[...], g, sh, sc).astype(BF16)
    if blocks_per_seq == 1:
        a_ref[0:HALO, :] = jnp.zeros((HALO, D), BF16)
        a_ref[HALO + tm:, :] = jnp.zeros((HALO, D), BF16)
    else:
        seq_first = (i % blocks_per_seq) == 0
        seq_last = (i % blocks_per_seq) == blocks_per_seq - 1
        a_ref[0:HALO, :] = jnp.where(seq_first, 0.0, _modulate(hp_ref[...], g, sh, sc)).astype(BF16)
        a_ref[HALO + tm:, :] = jnp.where(seq_last, 0.0, _modulate(hn_ref[...], g, sh, sc)).astype(BF16)

    def up(j):
        c0 = j * FFN_TF
        gb_ref[j % 2] = _dot(a_ref[...], wu_ref[:, c0:c0 + FFN_TF])
        vb_ref[j % 2] = _dot(a_ref[HALO:HALO + tm, :], wu_ref[:, FFN_DIM + c0:FFN_DIM + c0 + FFN_TF])

    up(0)
    for j in range(FFN_NF):
        if j + 1 < FFN_NF:
            up(j + 1)
        gb = gb_ref.at[j % 2]
        c0 = j * FFN_TF
        wc = wc_ref[:, c0:c0 + FFN_TF]
        gc = (gb[HALO:HALO + tm, :] * wc[1:2]
              + gb[HALO - 1:HALO - 1 + tm, :] * wc[0:1]
              + gb[HALO + 1:HALO + 1 + tm, :] * wc[2:3]
              + bc_ref[:, c0:c0 + FFN_TF])
        u = (gc * jax.nn.sigmoid(gc) * vb_ref[j % 2]).astype(BF16)
        y = _dot(u, wd_ref[c0:c0 + FFN_TF, :])
        if j == 0:
            o_ref[...] = y
        else:
            o_ref[...] += y

    out = h_ref[...] + gt_ref[...] * o_ref[...]
    if final_norm:
        ms = jnp.mean(out * out, axis=-1, keepdims=True)
        out = out * lax.rsqrt(ms + EPS) * gf_ref[...]
    o_ref[...] = out


def _ffn(h, mods, g_norm, wu, wc, bc, wd, g_final, *, tm, row0, n_compute, n_blocks, out_rows, in_place,
         final_norm):
    hb = tm // HALO
    b0 = row0 // tm
    last_halo = T // HALO - 1
    blocks_per_seq = L // tm if row0 == 0 else NCTX // tm
    assert not (in_place and blocks_per_seq != 1)
    kern = functools.partial(_ffn_kernel, tm=tm, blocks_per_seq=blocks_per_seq, n_compute=n_compute,
                             final_norm=final_norm)
    if blocks_per_seq == 1:
        halo = jnp.zeros((HALO, D), F32)
        halo_specs = [_const_spec((HALO, D)), _const_spec((HALO, D))]
    else:
        halo = h
        halo_specs = [pl.BlockSpec((HALO, D), lambda i: (jnp.maximum((b0 + i) * hb - 1, 0), 0)),
                      pl.BlockSpec((HALO, D), lambda i: (jnp.minimum((b0 + i + 1) * hb, last_halo), 0))]
    in_specs = [
        halo_specs[0],
        pl.BlockSpec((tm, D), lambda i: (b0 + i, 0)),
        halo_specs[1],
        _const_spec((1, D)),
        _mod_spec(3, tm, b0), _mod_spec(4, tm, b0), _mod_spec(5, tm, b0),
        _const_spec((D, 2 * FFN_DIM), single_buffer=True),
        _const_spec((3, FFN_DIM)),
        _const_spec((1, FFN_DIM)),
        _const_spec((FFN_DIM, D), single_buffer=True),
        _const_spec((1, D)),
    ]
    return pl.pallas_call(
        kern,
        out_shape=jax.ShapeDtypeStruct((out_rows, D), F32),
        grid=(n_blocks,),
        in_specs=in_specs,
        out_specs=pl.BlockSpec((tm, D), lambda i: (b0 + i, 0)),
        scratch_shapes=[pltpu.VMEM((tm + 2 * HALO, D), BF16), pltpu.VMEM((2, tm + 2 * HALO, FFN_TF), F32),
                        pltpu.VMEM((2, tm, FFN_TF), F32)],
        input_output_aliases={1: 0} if in_place else {},
        compiler_params=_cp(("arbitrary",)),
        name="conv_ffn_lat" if row0 == 0 else "conv_ffn_ctx",
    )(halo, h, halo, g_norm, mods, mods, mods, wu, wc, bc, wd, g_final)


def _modmm_kernel(h_ref, g_ref, sh_ref, sc_ref, w_ref, o_ref, *, tn):
    a = _modulate(h_ref[...], g_ref[...], sh_ref[...], sc_ref[...]).astype(BF16)
    for c in range(w_ref.shape[1] // tn):
        o_ref[:, c * tn:(c + 1) * tn] = _dot(a, w_ref[:, c * tn:(c + 1) * tn]).astype(BF16)


def _modmm(h, mods, g_norm, w, *, name):
    n = w.shape[1]
    return pl.pallas_call(
        functools.partial(_modmm_kernel, tn=512),
        out_shape=jax.ShapeDtypeStruct((T, n), BF16),
        grid=(T // TM,),
        in_specs=[
            pl.BlockSpec((TM, D), lambda i: (i, 0)),
            _const_spec((1, D)),
            _mod_spec(0, TM), _mod_spec(1, TM),
            _const_spec((D, n)),
        ],
        out_specs=pl.BlockSpec((TM, n), lambda i: (i, 0)),
        compiler_params=_cp(("arbitrary",)),
        name=name,
    )(h, g_norm, mods, mods, w)


def _outproj_kernel(h_ref, xl_ref, xc_ref, gt_ref, w_ref, o_ref, *, n_lat):
    i = pl.program_id(0)

    @pl.when(i < n_lat)
    def _():
        o_ref[...] = h_ref[...] + gt_ref[...] * _dot(xl_ref[...], w_ref[...])

    @pl.when(i >= n_lat)
    def _():
        o_ref[...] = h_ref[...] + gt_ref[...] * _dot(xc_ref[...], w_ref[...])


def _outproj(h, x_lat, x_ctx, mods, w, *, n_blocks, name):
    n_lat = T_LAT // TM
    return pl.pallas_call(
        functools.partial(_outproj_kernel, n_lat=n_lat),
        out_shape=jax.ShapeDtypeStruct((T, D), F32),
        grid=(n_blocks,),
        in_specs=[
            pl.BlockSpec((TM, D), lambda i: (i, 0)),
            pl.BlockSpec((TM, D), lambda i: (jnp.minimum(i, n_lat - 1), 0)),
            pl.BlockSpec((TM, D), lambda i: (jnp.maximum(i - n_lat, 0), 0)),
            _mod_spec(2, TM),
            _const_spec((D, D)),
        ],
        out_specs=pl.BlockSpec((TM, D), lambda i: (i, 0)),
        input_output_aliases={0: 0},
        compiler_params=_cp(("arbitrary",)),
        name=name,
    )(h, x_lat, x_ctx, mods, w)


def _fnet_chan_kernel(h_ref, g_ref, sh_ref, sc_ref, cs_ref, o_ref):
    a = _modulate(h_ref[...], g_ref[...], sh_ref[...], sc_ref[...]).astype(BF16)
    cs = cs_ref[...]
    for grp in range(FNET_GROUPS):
        pq = _dot(a[:, grp * FNET_GD:(grp + 1) * FNET_GD], cs)
        o_ref[0, :, grp * FNET_GD:(grp + 1) * FNET_GD] = pq[:, :FNET_GD].astype(BF16)
        o_ref[1, :, grp * FNET_GD:(grp + 1) * FNET_GD] = pq[:, FNET_GD:].astype(BF16)


def _fnet_chan(h, mods, g_norm, cs_chan):
    return pl.pallas_call(
        _fnet_chan_kernel,
        out_shape=jax.ShapeDtypeStruct((2, T, D), BF16),
        grid=(T // TM,),
        in_specs=[
            pl.BlockSpec((TM, D), lambda i: (i, 0)),
            _const_spec((1, D)),
            _mod_spec(0, TM), _mod_spec(1, TM),
            _const_spec((FNET_GD, 2 * FNET_GD)),
        ],
        out_specs=pl.BlockSpec((2, TM, D), lambda i: (0, i, 0)),
        compiler_params=_cp(("arbitrary",)),
        name="fnet_channel_dft",
    )(h, g_norm, mods, mods, cs_chan)


def _fnet_seq_kernel(tab_ref, pq_ref, h_ref, gt_ref, w_ref, o_ref, acc_ref, *, nk, inv_norm):
    k = pl.program_id(2)

    @pl.when(k == 0)
    def _():
        acc_ref[...] = jnp.zeros_like(acc_ref)

    acc_ref[...] += _dot(tab_ref[0], pq_ref[0])

    @pl.when(k == nk - 1)
    def _():
        f = (acc_ref[...] * inv_norm).astype(BF16)
        o_ref[...] = h_ref[...] + gt_ref[...] * _dot(f, w_ref[...])


def _fnet_seq(h, pq, table, mods, w, *, seq, row0, mod_row_fn, name):
    tm = min(TM, seq)
    tk = min(1024, seq)
    nkk = seq // tk
    nk = 2 * nkk
    mt = seq // tm
    rb0 = row0 // tm
    kb0 = row0 // tk
    kern = functools.partial(_fnet_seq_kernel, nk=nk, inv_norm=1.0 / np.sqrt(seq * FNET_GD))
    return pl.pallas_call(
        kern,
        out_shape=jax.ShapeDtypeStruct((T, D), F32),
        grid=(B, mt, nk),
        in_specs=[
            pl.BlockSpec((1, tm, tk), lambda b, m, k: (k // nkk, m, k % nkk)),
            pl.BlockSpec((1, tk, D), lambda b, m, k: (k // nkk, kb0 + b * nkk + k % nkk, 0)),
            pl.BlockSpec((tm, D), lambda b, m, k: (rb0 + b * mt + m, 0)),
            pl.BlockSpec((None, 1, D), lambda b, m, k: (mod_row_fn(b) * N_MOD + 2, 0, 0)),
            _const_spec((D, D)),
        ],
        out_specs=pl.BlockSpec((tm, D), lambda b, m, k: (rb0 + b * mt + m, 0)),
        scratch_shapes=[pltpu.VMEM((tm, D), F32)],
        input_output_aliases={2: 0},
        compiler_params=_cp(("arbitrary", "arbitrary", "arbitrary")),
        name=name,
    )(table, pq, h, mods, w)


def _dft_tables(n):
    r = int(np.sqrt(n))
    assert r * r == n
    t = jnp.arange(n, dtype=jnp.int32)[None, :]
    f = jnp.arange(r, dtype=jnp.int32)[:, None]
    a_hi = ((f * t) % r).astype(F32) * (2.0 * np.pi / r)
    a_lo = ((f * t) % n).astype(F32) * (2.0 * np.pi / n)
    c1, s1 = jnp.cos(a_hi), jnp.sin(a_hi)
    c0, s0 = jnp.cos(a_lo)[None, None], jnp.sin(a_lo)[None, None]
    tab = jnp.stack([c1, -s1])[:, :, None, :] * c0 + jnp.stack([-s1, -c1])[:, :, None, :] * s0
    return tab.reshape(2, n, n)


def _na_bias_table(rpb):
    n_co = 2 * NA_WIN_COLS - 1
    qc = np.arange(GRID_W)[:, None]
    kc = np.arange(GRID_W)[None, :]
    cs = np.clip(qc - NA_WIN_COLS // 2, 0, GRID_W - NA_WIN_COLS)
    col_ok = (kc >= cs) & (kc < cs + NA_WIN_COLS)
    col_sel = (kc - qc + NA_WIN_COLS - 1)[None] == np.arange(n_co)[:, None, None]
    by_col = jnp.einsum("hrd,dqk->hrqk", rpb * LOG2E, jnp.asarray(col_sel, F32), precision=lax.Precision.HIGHEST)
    a = jnp.where(col_ok, by_col, NEG)
    z = jnp.zeros((NA_HEADS, 1, GRID_W, GRID_W), F32)
    return jnp.concatenate([jnp.concatenate([z, a], axis=1), jnp.concatenate([a, z], axis=1)], axis=-1)


def _na_row_tables():
    rows = L // GRID_W
    nq = L // NA_TQ
    npair = NA_KROWS // 2
    idx = np.zeros((nq, NA_QROWS * npair), np.int32)
    rmask = np.zeros((nq, NA_QROWS * npair, LANES), np.float32)
    for g in range(nq):
        k0 = int(np.clip(NA_QROWS * g - NA_WIN_ROWS // 2, 0, rows - NA_KROWS))
        for qi in range(NA_QROWS):
            r = NA_QROWS * g + qi
            rs = int(np.clip(r - NA_WIN_ROWS // 2, 0, rows - NA_WIN_ROWS))
            for m in range(npair):
                kr = k0 + 2 * m
                idx[g, qi * npair + m] = int(np.clip(kr - r + NA_WIN_ROWS, 0, 2 * NA_WIN_ROWS - 1))
                for half in range(2):
                    if not rs <= kr + half < rs + NA_WIN_ROWS:
                        rmask[g, qi * npair + m, half * NA_DH:(half + 1) * NA_DH] = NEG
    return idx, rmask


def _na_kernel(idx_ref, q_ref, k_ref, v_ref, kc_ref, vc_ref, tab_ref, rmask_ref, o_ref):
    g = pl.program_id(2)
    k0 = jnp.clip(NA_QROWS * g - NA_WIN_ROWS // 2, 0, L // GRID_W - NA_KROWS)
    start = pl.multiple_of(k0 * GRID_W, GRID_W)
    lane = lax.broadcasted_iota(jnp.int32, (1, LANES), 1)
    lower = lane < NA_DH
    npair = NA_KROWS // 2

    for t in range(NA_HB):
        cols = slice(t * LANES, (t + 1) * LANES)
        q = q_ref[:, cols]
        kw = k_ref[pl.ds(start, NA_TK), cols]
        vw = v_ref[pl.ds(start, NA_TK), cols]
        kc = kc_ref[:, cols]
        vc = vc_ref[:, cols]
        outs = []
        for a in range(2):
            sel = (lane // NA_DH) == a
            qm = jnp.where(sel, q, jnp.zeros_like(q))
            bias = jnp.concatenate(
                [jnp.concatenate([tab_ref[2 * t + a, idx_ref[g, qi * npair + m]]
                                  + rmask_ref[qi * npair + m:qi * npair + m + 1, :] for m in range(npair)], axis=1)
                 for qi in range(NA_QROWS)], axis=0)
            s_nb = _dot_nt(qm, kw) + bias
            s_cx = _dot_nt(qm, kc)
            mx = jnp.maximum(jnp.max(s_nb, axis=-1, keepdims=True), jnp.max(s_cx, axis=-1, keepdims=True))
            o = (_dot(jnp.exp2(s_nb - mx).astype(BF16), jnp.where(sel, vw, jnp.ones_like(vw)))
                 + _dot(jnp.exp2(s_cx - mx).astype(BF16), jnp.where(sel, vc, jnp.ones_like(vc))))
            den = o[:, (1 - a) * NA_DH:(1 - a) * NA_DH + 1]
            outs.append(o * (1.0 / den))
        o_ref[:, cols] = jnp.where(lower, outs[0], outs[1]).astype(BF16)


def _na_attend(qkv, table):
    idx, rmask = _na_row_tables()
    wb = NA_HB * LANES
    nh = D // wb
    nq = L // NA_TQ
    return pl.pallas_call(
        _na_kernel,
        out_shape=jax.ShapeDtypeStruct((T_LAT, D), BF16),
        grid=(B, nh, nq),
        in_specs=[
            pl.BlockSpec(memory_space=pltpu.SMEM),
            pl.BlockSpec((NA_TQ, wb), lambda b, h, g: (b * nq + g, h)),
            pl.BlockSpec((L, wb), lambda b, h, g: (b, nh + h)),
            pl.BlockSpec((L, wb), lambda b, h, g: (b, 2 * nh + h)),
            pl.BlockSpec((NCTX, wb), lambda b, h, g: (T_LAT // NCTX + b, nh + h)),
            pl.BlockSpec((NCTX, wb), lambda b, h, g: (T_LAT // NCTX + b, 2 * nh + h)),
            pl.BlockSpec((2 * NA_HB, 2 * NA_WIN_ROWS, GRID_W, LANES), lambda b, h, g: (h, 0, 0, 0)),
            pl.BlockSpec((None, NA_QROWS * NA_KROWS // 2, LANES), lambda b, h, g: (g, 0, 0)),
        ],
        out_specs=pl.BlockSpec((NA_TQ, wb), lambda b, h, g: (b * nq + g, h)),
        compiler_params=_cp(("arbitrary", "arbitrary", "arbitrary")),
        name="na_attention",
    )(jnp.asarray(idx), qkv, qkv, qkv, qkv, qkv, table, jnp.asarray(rmask))


def _ctx_attn_kernel(q_ref, kc_ref, vc_ref, o_ref):
    kc = kc_ref[...]
    vc = vc_ref[...]
    q = q_ref[...]
    lane = lax.broadcasted_iota(jnp.int32, (1, LANES), 1)
    outs = []
    for a in range(2):
        qm = jnp.where((lane // NA_DH) == a, q, jnp.zeros_like(q))
        s = _dot_nt(qm, kc)
        p = jnp.exp2(s - jnp.max(s, axis=-1, keepdims=True))
        outs.append(_dot(p.astype(BF16), vc) / jnp.sum(p, axis=-1, keepdims=True))
    o_ref[...] = jnp.where(lane < NA_DH, outs[0], outs[1]).astype(BF16)


def _na_ctx_attend(qkv):
    hp = D // LANES
    cb = T_LAT // NCTX
    return pl.pallas_call(
        _ctx_attn_kernel,
        out_shape=jax.ShapeDtypeStruct((T_CTX, D), BF16),
        grid=(B, hp),
        in_specs=[
            pl.BlockSpec((NCTX, LANES), lambda b, h: (cb + b, h)),
            pl.BlockSpec((NCTX, LANES), lambda b, h: (cb + b, hp + h)),
            pl.BlockSpec((NCTX, LANES), lambda b, h: (cb + b, 2 * hp + h)),
        ],
        out_specs=pl.BlockSpec((NCTX, LANES), lambda b, h: (b, h)),
        compiler_params=_cp(("arbitrary", "arbitrary")),
        name="na_ctx_attention",
    )(qkv, qkv, qkv)


def _sg_kernel(h_ref, g_ref, sh_ref, sc_ref, gt_ref, win_ref, gv_ref, ws_ref, bs_ref, wout_ref, o_ref,
               u_ref, v_ref, t_ref):
    h = h_ref[...]
    a = _modulate(h, g_ref[...], sh_ref[...], sc_ref[...]).astype(BF16)
    tn = 512
    ssq = jnp.zeros((SG_TM, 1), F32)
    for c in range(2 * D // tn):
        z = jax.nn.gelu(_dot(a, win_ref[:, c * tn:(c + 1) * tn]), approximate=True)
        if c * tn < D:
            u_ref[:, c * tn:(c + 1) * tn] = z
        else:
            v_ref[:, c * tn - D:(c + 1) * tn - D] = z
            ssq = ssq + jnp.sum(z * z, axis=-1, keepdims=True)
    scale = lax.rsqrt(ssq * (1.0 / D) + EPS)
    vn = (v_ref[...] * scale * gv_ref[...]).astype(BF16)
    for ch in range(SG_TM // SG_CHUNK):
        r0 = ch * SG_CHUNK
        for grp in range(SG_GROUPS):
            c0 = grp * SG_GD
            mixed = _dot(ws_ref[grp], vn[r0:r0 + SG_CHUNK, c0:c0 + SG_GD])
            bs = bs_ref[grp]
            mixed = mixed + jnp.concatenate([bs, bs], axis=-1)
            t_ref[r0:r0 + SG_CHUNK, c0:c0 + SG_GD] = (u_ref[r0:r0 + SG_CHUNK, c0:c0 + SG_GD] * mixed).astype(BF16)
    o_ref[...] = h + gt_ref[...] * _dot(t_ref[...], wout_ref[...])


def _sg(h, mods, g_norm, w_in, g_v, w_s, b_s, w_out):
    return pl.pallas_call(
        _sg_kernel,
        out_shape=jax.ShapeDtypeStruct((T, D), F32),
        grid=(T // SG_TM,),
        in_specs=[
            pl.BlockSpec((SG_TM, D), lambda i: (i, 0)),
            _const_spec((1, D)),
            _mod_spec(0, SG_TM), _mod_spec(1, SG_TM), _mod_spec(2, SG_TM),
            _const_spec((D, 2 * D)),
            _const_spec((1, D)),
            _const_spec((SG_GROUPS, SG_CHUNK, SG_CHUNK)),
            _const_spec((SG_GROUPS, SG_CHUNK, LANES)),
            _const_spec((D, D)),
        ],
        out_specs=pl.BlockSpec((SG_TM, D), lambda i: (i, 0)),
        scratch_shapes=[pltpu.VMEM((SG_TM, D), F32), pltpu.VMEM((SG_TM, D), F32), pltpu.VMEM((SG_TM, D), BF16)],
        compiler_params=_cp(("arbitrary",)),
        name="spatial_gating",
    )(h, g_norm, mods, mods, mods, w_in, g_v, w_s, b_s, w_out)


def _gqa_qkv_kernel(h_ref, g_ref, sh_ref, sc_ref, w_ref, gq_ref, gk_ref, cos_ref, sin_ref,
                    q_ref, k_ref, v_ref):
    a = _modulate(h_ref[...], g_ref[...], sh_ref[...], sc_ref[...]).astype(BF16)
    cos = cos_ref[...]
    sin = sin_ref[...]
    lane = lax.broadcasted_iota(jnp.int32, (1, LANES), 1)
    head_lanes = lane < ATT_DH
    first_half = (lane % ATT_DH) < (ATT_DH // 2)
    tn = 256
    ri = lax.broadcasted_iota(jnp.int32, (tn, tn), 0) // ATT_DH
    ci = lax.broadcasted_iota(jnp.int32, (tn, tn), 1) // ATT_DH
    head_mean = jnp.where(ri == ci, 1.0 / ATT_DH, 0.0).astype(BF16)
    n_qk = (ATT_HEADS + ATT_KV) * ATT_DH
    n_q = ATT_HEADS * ATT_DH
    for c in range(w_ref.shape[1] // tn):
        y2 = _dot(a, w_ref[:, c * tn:(c + 1) * tn])
        if c * tn < n_qk:
            ms2 = _dot((y2 * y2).astype(BF16), head_mean)
        for s in range(tn // LANES):
            col = c * tn + s * LANES
            y = y2[:, s * LANES:(s + 1) * LANES]
            if col < n_qk:
                gain = gq_ref[...] if col < n_q else gk_ref[...]
                yn = y * lax.rsqrt(ms2[:, s * LANES:(s + 1) * LANES] + EPS) * gain
                partner = jnp.where(first_half, pltpu.roll(yn, LANES - ATT_DH // 2, 1), pltpu.roll(yn, ATT_DH // 2, 1))
                y = yn * cos + partner * sin
            if col < n_q:
                dst, base, fill = q_ref, col, 0.0
            elif col < n_qk:
                dst, base, fill = k_ref, col - n_q, 0.0
            else:
                dst, base, fill = v_ref, col - n_qk, 1.0
            dst[:, 2 * base:2 * base + LANES] = jnp.where(head_lanes, y, fill).astype(BF16)
            dst[:, 2 * base + LANES:2 * base + 2 * LANES] = jnp.where(
                head_lanes, pltpu.roll(y, ATT_DH, 1), fill).astype(BF16)


def _gqa_qkv(h, mods, g_norm, w, gq, gk, cos_t, sin_t):
    tab_spec = pl.BlockSpec((TM, LANES), lambda i: (jnp.where(i * TM >= T_LAT, L // TM, i % (L // TM)), 0))
    nq, nkv = ATT_HEADS * LANES, ATT_KV * LANES
    return pl.pallas_call(
        _gqa_qkv_kernel,
        out_shape=(jax.ShapeDtypeStruct((T, nq), BF16), jax.ShapeDtypeStruct((T, nkv), BF16),
                   jax.ShapeDtypeStruct((T, nkv), BF16)),
        grid=(T // TM,),
        in_specs=[
            pl.BlockSpec((TM, D), lambda i: (i, 0)),
            _const_spec((1, D)),
            _mod_spec(0, TM), _mod_spec(1, TM),
            _const_spec(w.shape),
            _const_spec((1, LANES)), _const_spec((1, LANES)),
            tab_spec, tab_spec,
        ],
        out_specs=(pl.BlockSpec((TM, nq), lambda i: (i, 0)), pl.BlockSpec((TM, nkv), lambda i: (i, 0)),
                   pl.BlockSpec((TM, nkv), lambda i: (i, 0))),
        compiler_params=_cp(("arbitrary",)),
        name="gqa_qkv_rope",
    )(h, g_norm, mods, mods, w, gq, gk, cos_t, sin_t)


def _gqa_kernel(q_ref, k_ref, v_ref, kc_ref, vc_ref, o_ref):
    lane = lax.broadcasted_iota(jnp.int32, (1, LANES), 1)
    nh = ATT_HEADS // ATT_KV
    qs = jnp.concatenate([q_ref[:, i * LANES:(i + 1) * LANES] for i in range(nh)], axis=0)
    chunks = [(k_ref, v_ref, c * GQ_CK, GQ_CK) for c in range(L // GQ_CK)] + [(kc_ref, vc_ref, 0, NCTX)]
    m = jnp.full((nh * GQ_TQ, 1), -jnp.inf, F32)
    acc = jnp.zeros((nh * GQ_TQ, LANES), F32)
    for kr, vr, c0, cn in chunks:
        s = _dot_nt(qs, kr[c0:c0 + cn, :])
        m_new = jnp.maximum(m, jnp.max(s, axis=-1, keepdims=True))
        p = jnp.exp2(s - m_new).astype(BF16)
        acc = jnp.exp2(m - m_new) * acc + _dot(p, vr[c0:c0 + cn, :])
        m = m_new
    o = acc * (1.0 / acc[:, ATT_DH:ATT_DH + 1])
    for j in range(nh // 2):
        oa = o[2 * j * GQ_TQ:(2 * j + 1) * GQ_TQ]
        ob = o[(2 * j + 1) * GQ_TQ:(2 * j + 2) * GQ_TQ]
        o_ref[:, j * LANES:(j + 1) * LANES] = jnp.where(lane < ATT_DH, oa, pltpu.roll(ob, ATT_DH, 1)).astype(BF16)


def _gqa_attend(q, k, v):
    nh = ATT_HEADS // ATT_KV
    nqt = L // GQ_TQ
    cb = T_LAT // NCTX
    return pl.pallas_call(
        _gqa_kernel,
        out_shape=jax.ShapeDtypeStruct((T_LAT, D), BF16),
        grid=(B, ATT_KV, nqt),
        in_specs=[
            pl.BlockSpec((GQ_TQ, nh * LANES), lambda b, h, t: (b * nqt + t, h)),
            pl.BlockSpec((L, LANES), lambda b, h, t: (b, h)),
            pl.BlockSpec((L, LANES), lambda b, h, t: (b, h)),
            pl.BlockSpec((NCTX, LANES), lambda b, h, t: (cb + b, h)),
            pl.BlockSpec((NCTX, LANES), lambda b, h, t: (cb + b, h)),
        ],
        out_specs=pl.BlockSpec((GQ_TQ, nh * ATT_DH), lambda b, h, t: (b * nqt + t, h)),
        compiler_params=_cp(("arbitrary", "arbitrary", "arbitrary")),
        name="gqa_attention",
    )(q, k, v, k, v)


def _rope_tables():
    t = jnp.arange(L)
    row = (t // GRID_W).astype(F32)
    col = (t % GRID_W).astype(F32)
    n_freq = ATT_DH // 4
    inv_freq = ROPE_THETA ** (-jnp.arange(n_freq, dtype=F32) / n_freq)
    ang = jnp.concatenate([row[:, None] * inv_freq, col[:, None] * inv_freq], axis=-1)
    cos, sin = jnp.cos(ang), jnp.sin(ang)
    cos_t = jnp.concatenate([cos, cos, cos, cos], axis=-1)
    sin_t = jnp.concatenate([-sin, sin, -sin, sin], axis=-1)
    cos_t = jnp.concatenate([cos_t, jnp.ones((TM, LANES), F32)], axis=0)
    sin_t = jnp.concatenate([sin_t, jnp.zeros((TM, LANES), F32)], axis=0)
    return cos_t, sin_t


def kernel(x, c, ctx, c_ctx, w_mod, b_mod, g_norm_mix, g_norm_ffn, w_ffn_up, w_ffn_conv, b_ffn_conv, w_ffn_down,
           w_fnet_out, w_na_qkv, na_rel_bias, w_na_out, w_sg_in, g_sg_v, w_sg_spatial, b_sg_spatial, w_sg_out,
           w_att_qkv, g_att_q, g_att_k, w_att_out, g_final):
    assert x.shape == (B, L, D) and ctx.shape == (B, NCTX, D) and w_mod.shape[0] == DEPTH == 4

    h = jnp.concatenate([x.reshape(T_LAT, D), ctx.reshape(T_CTX, D)], axis=0)
    cond8 = jnp.concatenate([c, c_ctx[None, :], jnp.zeros((8 - B - 1, D), F32)], axis=0)
    mods_all = _ada(cond8, w_mod, b_mod)

    w_up_bf = w_ffn_up.astype(BF16)
    w_down_bf = w_ffn_down.astype(BF16)

    def layer_mods(i):
        return mods_all[i, :B + 1].reshape((B + 1) * N_MOD, 1, D)

    def ffn(h, i, *, last):
        args = (layer_mods(i), g_norm_ffn[i][None, :], w_up_bf[i], w_ffn_conv[i], b_ffn_conv[i][None, :], w_down_bf[i],
                g_final[None, :])
        nl = T_LAT // TM
        if last:
            return _ffn(h, *args, tm=TM, row0=0, n_compute=nl, n_blocks=nl, out_rows=T_LAT, in_place=False,
                        final_norm=True)
        h = _ffn(h, *args, tm=NCTX, row0=T_LAT, n_compute=B, n_blocks=B, out_rows=T, in_place=True, final_norm=False)
        return _ffn(h, *args, tm=TM, row0=0, n_compute=nl, n_blocks=T // TM, out_rows=T, in_place=False,
                    final_norm=False)

    mods = layer_mods(0)
    tab_c = _dft_tables(FNET_GD)
    cs_chan = jnp.concatenate([tab_c[0], -tab_c[1]], axis=1).astype(BF16)
    pq = _fnet_chan(h, mods, g_norm_mix[0][None, :], cs_chan)
    tab_lat = _dft_tables(L).astype(BF16)
    tab_ctx = tab_c.astype(BF16)
    w_f = w_fnet_out[0].astype(BF16)
    h = _fnet_seq(h, pq, tab_lat, mods, w_f, seq=L, row0=0, mod_row_fn=lambda b: b, name="fnet_seq_lat")
    h = _fnet_seq(h, pq, tab_ctx, mods, w_f, seq=NCTX, row0=T_LAT, mod_row_fn=lambda b: B, name="fnet_seq_ctx")
    h = ffn(h, 0, last=False)

    mods = layer_mods(1)
    scale = NA_DH ** -0.5 * LOG2E
    w_qkv = jnp.concatenate([w_na_qkv[0][:, :D] * scale, w_na_qkv[0][:, D:]], axis=1).astype(BF16)
    qkv = _modmm(h, mods, g_norm_mix[1][None, :], w_qkv, name="na_qkv")
    o_lat = _na_attend(qkv, _na_bias_table(na_rel_bias[0]))
    o_ctx = _na_ctx_attend(qkv)
    h = _outproj(h, o_lat, o_ctx, mods, w_na_out[0].astype(BF16), n_blocks=T // TM, name="na_out")
    h = ffn(h, 1, last=False)

    mods = layer_mods(2)
    b_s = jnp.broadcast_to(b_sg_spatial[0][:, :, None], (SG_GROUPS, SG_CHUNK, LANES))
    h = _sg(h, mods, g_norm_mix[2][None, :], w_sg_in[0].astype(BF16), g_sg_v[0][None, :],
            w_sg_spatial[0].astype(BF16), b_s, w_sg_out[0].astype(BF16))
    h = ffn(h, 2, last=False)

    mods = layer_mods(3)
    gq = jnp.tile(g_att_q[0] * (ATT_DH ** -0.5 * LOG2E), LANES // ATT_DH)[None, :]
    gk = jnp.tile(g_att_k[0], LANES // ATT_DH)[None, :]
    cos_t, sin_t = _rope_tables()
    q, k, v = _gqa_qkv(h, mods, g_norm_mix[3][None, :], w_att_qkv[0].astype(BF16), gq, gk, cos_t, sin_t)
    o = _gqa_attend(q, k, v)
    h = _outproj(h, o, o, mods, w_att_out[0].astype(BF16), n_blocks=T_LAT // TM, name="gqa_out")
    out = ffn(h, 3, last=True)
    return out.reshape(B, L, D)
```

```python
import functools

import numpy as np
import jax
import jax.numpy as jnp
from jax import lax
from jax.experimental import pallas as pl
from jax.experimental.pallas import tpu as pltpu

F32 = jnp.float32
BF16 = jnp.bfloat16

D = 1024
B = 4
L = 4096
NCTX = 256
DEPTH = 4
GRID_W = 64
EPS = 1e-6
N_MOD = 6
T_LAT = B * L
T_CTX = B * NCTX
T = T_LAT + T_CTX

FNET_GROUPS = 4
FNET_GD = D // FNET_GROUPS

NA_HEADS = 16
NA_DH = D // NA_HEADS
NA_WIN_ROWS = 8
NA_WIN_COLS = 16
NA_QROWS = 4
NA_KROWS = 12
NA_TQ = NA_QROWS * GRID_W
NA_TK = NA_KROWS * GRID_W
NA_HB = 4

SG_CHUNK = 128
SG_GROUPS = 4
SG_GD = D // SG_GROUPS

ATT_HEADS = 16
ATT_KV = 4
ATT_DH = 64
ROPE_THETA = 10000.0
GQ_TQ = 256
GQ_CK = 256

FFN_DIM = 2816
FFN_TF = 256
FFN_NF = FFN_DIM // FFN_TF
HALO = 16

TM = 1024
SG_TM = 512
LANES = 128
NEG = -1e30
LOG2E = float(np.log2(np.e))

VMEM_LIMIT = 56 * 1024 * 1024


def _cp(sem, vmem=VMEM_LIMIT):
    return pltpu.CompilerParams(dimension_semantics=sem, vmem_limit_bytes=vmem)


def _mod_row(i, tm):
    return (i * tm) // L


def _mod_spec(k, tm, b0=0):
    return pl.BlockSpec((None, 1, D), lambda i: (_mod_row(b0 + i, tm) * N_MOD + k, 0, 0))


def _const_spec(shape, single_buffer=False):
    nd = len(shape)
    if single_buffer:
        return pl.BlockSpec(shape, lambda *_: (0,) * nd, pipeline_mode=pl.Buffered(1))
    return pl.BlockSpec(shape, lambda *_: (0,) * nd)


def _modulate(x, g, shift, scale):
    ms = jnp.mean(x * x, axis=-1, keepdims=True)
    return x * lax.rsqrt(ms + EPS) * (g * (1.0 + scale)) + shift


def _dot(a, b):
    return jnp.dot(a, b, preferred_element_type=F32)


def _dot_nt(a, b):
    return lax.dot_general(a, b, (((1,), (1,)), ((), ())), preferred_element_type=F32)


def _ada_kernel(c_ref, w_ref, b_ref, o_ref):
    c = c_ref[...]
    s = (c * jax.nn.sigmoid(c)).astype(BF16)
    o_ref[0] = _dot(s, w_ref[0].astype(BF16)) + b_ref[0]


def _ada(cond8, w_mod, b_mod):
    tn = 2048
    n = N_MOD * D
    return pl.pallas_call(
        _ada_kernel,
        out_shape=jax.ShapeDtypeStruct((DEPTH, 8, n), F32),
        grid=(DEPTH, n // tn),
        in_specs=[
            pl.BlockSpec((8, D), lambda l, j: (0, 0)),
            pl.BlockSpec((1, D, tn), lambda l, j: (l, 0, j)),
            pl.BlockSpec((1, 1, tn), lambda l, j: (l, 0, j)),
        ],
        out_specs=pl.BlockSpec((1, 8, tn), lambda l, j: (l, 0, j)),
        compiler_params=_cp(("arbitrary", "arbitrary")),
        name="ada_params",
    )(cond8, w_mod, b_mod.reshape(DEPTH, 1, n))


def _ffn_kernel(hp_ref, h_ref, hn_ref, g_ref, sh_ref, sc_ref, gt_ref, wu_ref, wc_ref, bc_ref,
                wd_ref, gf_ref, o_ref, a_ref, gb_ref, vb_ref, *, tm, blocks_per_seq, n_compute, final_norm):
    i = pl.program_id(0)

    @pl.when(i >= n_compute)
    def _():
        o_ref[...] = h_ref[...]

    @pl.when(i < n_compute)
    def _():
        _ffn_block(hp_ref, h_ref, hn_ref, g_ref, sh_ref, sc_ref, gt_ref, wu_ref, wc_ref, bc_ref, wd_ref,
                   gf_ref, o_ref, a_ref, gb_ref, vb_ref, tm=tm, blocks_per_seq=blocks_per_seq, final_norm=final_norm)


def _ffn_block(hp_ref, h_ref, hn_ref, g_ref, sh_ref, sc_ref, gt_ref, wu_ref, wc_ref, bc_ref, wd_ref,
               gf_ref, o_ref, a_ref, gb_ref, vb_ref, *, tm, blocks_per_seq, final_norm):
    i = pl.program_id(0)
    g = g_ref[...]
    sh = sh_ref[...]
    sc = sc_ref[...]

    a_ref[HALO:HALO + tm, :] = _modulate(h_ref[...], g, sh, sc).astype(BF16)
    if blocks_per_seq == 1:
        a_ref[0:HALO, :] = jnp.zeros((HALO, D), BF16)
        a_ref[HALO + tm:, :] = jnp.zeros((HALO, D), BF16)
    else:
        seq_first = (i % blocks_per_seq) == 0
        seq_last = (i % blocks_per_seq) == blocks_per_seq - 1
        a_ref[0:HALO, :] = jnp.where(seq_first, 0.0, _modulate(hp_ref[...], g, sh, sc)).astype(BF16)
        a_ref[HALO + tm:, :] = jnp.where(seq_last, 0.0, _modulate(hn_ref[...], g, sh, sc)).astype(BF16)

    def up(j):
        c0 = j * FFN_TF
        gb_ref[j % 2] = _dot(a_ref[...], wu_ref[:, c0:c0 + FFN_TF])
        vb_ref[j % 2] = _dot(a_ref[HALO:HALO + tm, :], wu_ref[:, FFN_DIM + c0:FFN_DIM + c0 + FFN_TF])

    up(0)
    for j in range(FFN_NF):
        if j + 1 < FFN_NF:
            up(j + 1)
        gb = gb_ref.at[j % 2]
        c0 = j * FFN_TF
        wc = wc_ref[:, c0:c0 + FFN_TF]
        gc = (gb[HALO:HALO + tm, :] * wc[1:2]
              + gb[HALO - 1:HALO - 1 + tm, :] * wc[0:1]
              + gb[HALO + 1:HALO + 1 + tm, :] * wc[2:3]
              + bc_ref[:, c0:c0 + FFN_TF])
        u = (gc * jax.nn.sigmoid(gc) * vb_ref[j % 2]).astype(BF16)
        y = _dot(u, wd_ref[c0:c0 + FFN_TF, :])
        if j == 0:
            o_ref[...] = y
        else:
            o_ref[...] += y

    out = h_ref[...] + gt_ref[...] * o_ref[...]
    if final_norm:
        ms = jnp.mean(out * out, axis=-1, keepdims=True)
        out = out * lax.rsqrt(ms + EPS) * gf_ref[...]
    o_ref[...] = out


def _ffn(h, mods, g_norm, wu, wc, bc, wd, g_final, *, tm, row0, n_compute, n_blocks, out_rows, in_place,
         final_norm):
    hb = tm // HALO
    b0 = row0 // tm
    last_halo = T // HALO - 1
    blocks_per_seq = L // tm if row0 == 0 else NCTX // tm
    assert not (in_place and blocks_per_seq != 1)
    kern = functools.partial(_ffn_kernel, tm=tm, blocks_per_seq=blocks_per_seq, n_compute=n_compute,
                             final_norm=final_norm)
    if blocks_per_seq == 1:
        halo = jnp.zeros((HALO, D), F32)
        halo_specs = [_const_spec((HALO, D)), _const_spec((HALO, D))]
    else:
        halo = h
        halo_specs = [pl.BlockSpec((HALO, D), lambda i: (jnp.maximum((b0 + i) * hb - 1, 0), 0)),
                      pl.BlockSpec((HALO, D), lambda i: (jnp.minimum((b0 + i + 1) * hb, last_halo), 0))]
    in_specs = [
        halo_specs[0],
        pl.BlockSpec((tm, D), lambda i: (b0 + i, 0)),
        halo_specs[1],
        _const_spec((1, D)),
        _mod_spec(3, tm, b0), _mod_spec(4, tm, b0), _mod_spec(5, tm, b0),
        _const_spec((D, 2 * FFN_DIM), single_buffer=True),
        _const_spec((3, FFN_DIM)),
        _const_spec((1, FFN_DIM)),
        _const_spec((FFN_DIM, D), single_buffer=True),
        _const_spec((1, D)),
    ]
    return pl.pallas_call(
        kern,
        out_shape=jax.ShapeDtypeStruct((out_rows, D), F32),
        grid=(n_blocks,),
        in_specs=in_specs,
        out_specs=pl.BlockSpec((tm, D), lambda i: (b0 + i, 0)),
        scratch_shapes=[pltpu.VMEM((tm + 2 * HALO, D), BF16), pltpu.VMEM((2, tm + 2 * HALO, FFN_TF), F32),
                        pltpu.VMEM((2, tm, FFN_TF), F32)],
        input_output_aliases={1: 0} if in_place else {},
        compiler_params=_cp(("arbitrary",)),
        name="conv_ffn_lat" if row0 == 0 else "conv_ffn_ctx",
    )(halo, h, halo, g_norm, mods, mods, mods, wu, wc, bc, wd, g_final)


def _modmm_kernel(h_ref, g_ref, sh_ref, sc_ref, w_ref, o_ref, *, tn):
    a = _modulate(h_ref[...], g_ref[...], sh_ref[...], sc_ref[...]).astype(BF16)
    for c in range(w_ref.shape[1] // tn):
        o_ref[:, c * tn:(c + 1) * tn] = _dot(a, w_ref[:, c * tn:(c + 1) * tn]).astype(BF16)


def _modmm(h, mods, g_norm, w, *, name):
    n = w.shape[1]
    return pl.pallas_call(
        functools.partial(_modmm_kernel, tn=512),
        out_shape=jax.ShapeDtypeStruct((T, n), BF16),
        grid=(T // TM,),
        in_specs=[
            pl.BlockSpec((TM, D), lambda i: (i, 0)),
            _const_spec((1, D)),
            _mod_spec(0, TM), _mod_spec(1, TM),
            _const_spec((D, n)),
        ],
        out_specs=pl.BlockSpec((TM, n), lambda i: (i, 0)),
        compiler_params=_cp(("arbitrary",)),
        name=name,
    )(h, g_norm, mods, mods, w)


def _outproj_kernel(h_ref, xl_ref, xc_ref, gt_ref, w_ref, o_ref, *, n_lat):
    i = pl.program_id(0)

    @pl.when(i < n_lat)
    def _():
        o_ref[...] = h_ref[...] + gt_ref[...] * _dot(xl_ref[...], w_ref[...])

    @pl.when(i >= n_lat)
    def _():
        o_ref[...] = h_ref[...] + gt_ref[...] * _dot(xc_ref[...], w_ref[...])


def _outproj(h, x_lat, x_ctx, mods, w, *, n_blocks, name):
    n_lat = T_LAT // TM
    return pl.pallas_call(
        functools.partial(_outproj_kernel, n_lat=n_lat),
        out_shape=jax.ShapeDtypeStruct((T, D), F32),
        grid=(n_blocks,),
        in_specs=[
            pl.BlockSpec((TM, D), lambda i: (i, 0)),
            pl.BlockSpec((TM, D), lambda i: (jnp.minimum(i, n_lat - 1), 0)),
            pl.BlockSpec((TM, D), lambda i: (jnp.maximum(i - n_lat, 0), 0)),
            _mod_spec(2, TM),
            _const_spec((D, D)),
        ],
        out_specs=pl.BlockSpec((TM, D), lambda i: (i, 0)),
        input_output_aliases={0: 0},
        compiler_params=_cp(("arbitrary",)),
        name=name,
    )(h, x_lat, x_ctx, mods, w)


def _fnet_chan_kernel(h_ref, g_ref, sh_ref, sc_ref, cs_ref, o_ref):
    a = _modulate(h_ref[...], g_ref[...], sh_ref[...], sc_ref[...]).astype(BF16)
    cs = cs_ref[...]
    for grp in range(FNET_GROUPS):
        pq = _dot(a[:, grp * FNET_GD:(grp + 1) * FNET_GD], cs)
        o_ref[0, :, grp * FNET_GD:(grp + 1) * FNET_GD] = pq[:, :FNET_GD].astype(BF16)
        o_ref[1, :, grp * FNET_GD:(grp + 1) * FNET_GD] = pq[:, FNET_GD:].astype(BF16)


def _fnet_chan(h, mods, g_norm, cs_chan):
    return pl.pallas_call(
        _fnet_chan_kernel,
        out_shape=jax.ShapeDtypeStruct((2, T, D), BF16),
        grid=(T // TM,),
        in_specs=[
            pl.BlockSpec((TM, D), lambda i: (i, 0)),
            _const_spec((1, D)),
            _mod_spec(0, TM), _mod_spec(1, TM),
            _const_spec((FNET_GD, 2 * FNET_GD)),
        ],
        out_specs=pl.BlockSpec((2, TM, D), lambda i: (0, i, 0)),
        compiler_params=_cp(("arbitrary",)),
        name="fnet_channel_dft",
    )(h, g_norm, mods, mods, cs_chan)


def _fnet_seq_kernel(hi_ref, lo_ref, pq_ref, h_ref, gt_ref, w_ref, o_ref, acc_ref, tab_ref, *, r, nkk, inv_norm):
    k = pl.program_id(2)
    use_cos = k < nkk
    c1, s1 = hi_ref[0], hi_ref[1]
    c0, s0 = lo_ref[0], lo_ref[1]
    a1 = jnp.where(use_cos, c1, -s1)
    b1 = jnp.where(use_cos, -s1, -c1)
    for fh in range(c1.shape[0]):
        tab_ref[fh * r:(fh + 1) * r, :] = (a1[fh:fh + 1, :] * c0 + b1[fh:fh + 1, :] * s0).astype(BF16)
    y = _dot(tab_ref[...], pq_ref[0])

    @pl.when(k == 0)
    def _():
        acc_ref[...] = y

    @pl.when((k > 0) & (k < 2 * nkk - 1))
    def _():
        acc_ref[...] += y

    @pl.when(k == 2 * nkk - 1)
    def _():
        f = ((acc_ref[...] + y) * inv_norm).astype(BF16)
        o_ref[...] = h_ref[...] + gt_ref[...] * _dot(f, w_ref[...])


def _fnet_seq(h, pq, angles, mods, w, *, seq, row0, mod_row_fn, name):
    hi, lo = angles
    r = lo.shape[1]
    tm = min(TM, seq)
    tk = min(2048, seq)
    nkk = seq // tk
    mt = seq // tm
    rb0 = row0 // tm
    kb0 = row0 // tk
    kern = functools.partial(_fnet_seq_kernel, r=r, nkk=nkk, inv_norm=1.0 / np.sqrt(seq * FNET_GD))
    return pl.pallas_call(
        kern,
        out_shape=jax.ShapeDtypeStruct((T, D), F32),
        grid=(B, mt, 2 * nkk),
        in_specs=[
            pl.BlockSpec((2, tm // r, tk), lambda b, m, k: (0, m, k % nkk)),
            pl.BlockSpec((2, r, tk), lambda b, m, k: (0, 0, k % nkk)),
            pl.BlockSpec((1, tk, D), lambda b, m, k: (k // nkk, kb0 + b * nkk + k % nkk, 0)),
            pl.BlockSpec((tm, D), lambda b, m, k: (rb0 + b * mt + m, 0)),
            pl.BlockSpec((None, 1, D), lambda b, m, k: (mod_row_fn(b) * N_MOD + 2, 0, 0)),
            _const_spec((D, D)),
        ],
        out_specs=pl.BlockSpec((tm, D), lambda b, m, k: (rb0 + b * mt + m, 0)),
        scratch_shapes=[pltpu.VMEM((tm, D), F32), pltpu.VMEM((tm, tk), BF16)],
        input_output_aliases={3: 0},
        compiler_params=_cp(("arbitrary", "arbitrary", "arbitrary")),
        name=name,
    )(hi, lo, pq, h, mods, w)


def _dft_angles(n):
    r = int(np.sqrt(n))
    assert r * r == n
    t = jnp.arange(n, dtype=jnp.int32)[None, :]
    f = jnp.arange(r, dtype=jnp.int32)[:, None]
    a_hi = ((f * t) % r).astype(F32) * (2.0 * np.pi / r)
    a_lo = ((f * t) % n).astype(F32) * (2.0 * np.pi / n)
    return jnp.stack([jnp.cos(a_hi), jnp.sin(a_hi)]), jnp.stack([jnp.cos(a_lo), jnp.sin(a_lo)])


def _na_bias_table(rpb):
    n_co = 2 * NA_WIN_COLS - 1
    qc = np.arange(GRID_W)[:, None]
    kc = np.arange(GRID_W)[None, :]
    cs = np.clip(qc - NA_WIN_COLS // 2, 0, GRID_W - NA_WIN_COLS)
    col_ok = (kc >= cs) & (kc < cs + NA_WIN_COLS)
    col_sel = (kc - qc + NA_WIN_COLS - 1)[None] == np.arange(n_co)[:, None, None]
    by_col = jnp.einsum("hrd,dqk->hrqk", rpb * LOG2E, jnp.asarray(col_sel, F32), precision=lax.Precision.HIGHEST)
    a = jnp.where(col_ok, by_col, NEG)
    z = jnp.zeros((NA_HEADS, 1, GRID_W, GRID_W), F32)
    return jnp.concatenate([jnp.concatenate([z, a], axis=1), jnp.concatenate([a, z], axis=1)], axis=-1)


def _na_row_tables():
    rows = L // GRID_W
    nq = L // NA_TQ
    npair = NA_KROWS // 2
    idx = np.zeros((nq, NA_QROWS * npair), np.int32)
    rmask = np.zeros((nq, NA_QROWS * npair, LANES), np.float32)
    for g in range(nq):
        k0 = int(np.clip(NA_QROWS * g - NA_WIN_ROWS // 2, 0, rows - NA_KROWS))
        for qi in range(NA_QROWS):
            r = NA_QROWS * g + qi
            rs = int(np.clip(r - NA_WIN_ROWS // 2, 0, rows - NA_WIN_ROWS))
            for m in range(npair):
                kr = k0 + 2 * m
                idx[g, qi * npair + m] = int(np.clip(kr - r + NA_WIN_ROWS, 0, 2 * NA_WIN_ROWS - 1))
                for half in range(2):
                    if not rs <= kr + half < rs + NA_WIN_ROWS:
                        rmask[g, qi * npair + m, half * NA_DH:(half + 1) * NA_DH] = NEG
    return idx, rmask


def _na_kernel(idx_ref, q_ref, k_ref, v_ref, kc_ref, vc_ref, tab_ref, rmask_ref, o_ref, snb_ref, scx_ref):
    g = pl.program_id(2)
    k0 = jnp.clip(NA_QROWS * g - NA_WIN_ROWS // 2, 0, L // GRID_W - NA_KROWS)
    start = pl.multiple_of(k0 * GRID_W, GRID_W)
    lane = lax.broadcasted_iota(jnp.int32, (1, LANES), 1)
    lower = lane < NA_DH
    npair = NA_KROWS // 2

    heads = [(t, a) for t in range(NA_HB) for a in range(2)]

    def scores(i):
        t, a = heads[i]
        cols = slice(t * LANES, (t + 1) * LANES)
        q = q_ref[:, cols]
        qm = jnp.where((lane // NA_DH) == a, q, jnp.zeros_like(q))
        bias = jnp.concatenate(
            [jnp.concatenate([tab_ref[2 * t + a, idx_ref[g, qi * npair + m]]
                              + rmask_ref[qi * npair + m:qi * npair + m + 1, :] for m in range(npair)], axis=1)
             for qi in range(NA_QROWS)], axis=0)
        snb_ref[i % 2] = _dot_nt(qm, k_ref[pl.ds(start, NA_TK), cols]) + bias
        scx_ref[i % 2] = _dot_nt(qm, kc_ref[:, cols])

    scores(0)
    prev = None
    for i, (t, a) in enumerate(heads):
        if i + 1 < len(heads):
            scores(i + 1)
        cols = slice(t * LANES, (t + 1) * LANES)
        sel = (lane // NA_DH) == a
        s_nb = snb_ref[i % 2]
        s_cx = scx_ref[i % 2]
        mx = jnp.maximum(jnp.max(s_nb, axis=-1, keepdims=True), jnp.max(s_cx, axis=-1, keepdims=True))
        vw = v_ref[pl.ds(start, NA_TK), cols]
        vc = vc_ref[:, cols]
        o = (_dot(jnp.exp2(s_nb - mx).astype(BF16), jnp.where(sel, vw, jnp.ones_like(vw)))
             + _dot(jnp.exp2(s_cx - mx).astype(BF16), jnp.where(sel, vc, jnp.ones_like(vc))))
        o = o * (1.0 / o[:, (1 - a) * NA_DH:(1 - a) * NA_DH + 1])
        if a == 0:
            prev = o
        else:
            o_ref[:, cols] = jnp.where(lower, prev, o).astype(BF16)


def _na_attend(qkv, table):
    idx, rmask = _na_row_tables()
    wb = NA_HB * LANES
    nh = D // wb
    nq = L // NA_TQ
    return pl.pallas_call(
        _na_kernel,
        out_shape=jax.ShapeDtypeStruct((T_LAT, D), BF16),
        grid=(B, nh, nq),
        in_specs=[
            pl.BlockSpec(memory_space=pltpu.SMEM),
            pl.BlockSpec((NA_TQ, wb), lambda b, h, g: (b * nq + g, h)),
            pl.BlockSpec((L, wb), lambda b, h, g: (b, nh + h)),
            pl.BlockSpec((L, wb), lambda b, h, g: (b, 2 * nh + h)),
            pl.BlockSpec((NCTX, wb), lambda b, h, g: (T_LAT // NCTX + b, nh + h)),
            pl.BlockSpec((NCTX, wb), lambda b, h, g: (T_LAT // NCTX + b, 2 * nh + h)),
            pl.BlockSpec((2 * NA_HB, 2 * NA_WIN_ROWS, GRID_W, LANES), lambda b, h, g: (h, 0, 0, 0)),
            pl.BlockSpec((None, NA_QROWS * NA_KROWS // 2, LANES), lambda b, h, g: (g, 0, 0)),
        ],
        out_specs=pl.BlockSpec((NA_TQ, wb), lambda b, h, g: (b * nq + g, h)),
        scratch_shapes=[pltpu.VMEM((2, NA_TQ, NA_TK), F32), pltpu.VMEM((2, NA_TQ, NCTX), F32)],
        compiler_params=_cp(("arbitrary", "arbitrary", "arbitrary")),
        name="na_attention",
    )(jnp.asarray(idx), qkv, qkv, qkv, qkv, qkv, table, jnp.asarray(rmask))


def _ctx_attn_kernel(q_ref, kc_ref, vc_ref, o_ref):
    kc = kc_ref[...]
    vc = vc_ref[...]
    q = q_ref[...]
    lane = lax.broadcasted_iota(jnp.int32, (1, LANES), 1)
    outs = []
    for a in range(2):
        qm = jnp.where((lane // NA_DH) == a, q, jnp.zeros_like(q))
        s = _dot_nt(qm, kc)
        p = jnp.exp2(s - jnp.max(s, axis=-1, keepdims=True))
        outs.append(_dot(p.astype(BF16), vc) / jnp.sum(p, axis=-1, keepdims=True))
    o_ref[...] = jnp.where(lane < NA_DH, outs[0], outs[1]).astype(BF16)


def _na_ctx_attend(qkv):
    hp = D // LANES
    cb = T_LAT // NCTX
    return pl.pallas_call(
        _ctx_attn_kernel,
        out_shape=jax.ShapeDtypeStruct((T_CTX, D), BF16),
        grid=(B, hp),
        in_specs=[
            pl.BlockSpec((NCTX, LANES), lambda b, h: (cb + b, h)),
            pl.BlockSpec((NCTX, LANES), lambda b, h: (cb + b, hp + h)),
            pl.BlockSpec((NCTX, LANES), lambda b, h: (cb + b, 2 * hp + h)),
        ],
        out_specs=pl.BlockSpec((NCTX, LANES), lambda b, h: (b, h)),
        compiler_params=_cp(("arbitrary", "arbitrary")),
        name="na_ctx_attention",
    )(qkv, qkv, qkv)


def _sg_kernel(h_ref, g_ref, sh_ref, sc_ref, gt_ref, win_ref, gv_ref, ws_ref, bs_ref, wout_ref, o_ref,
               u_ref, v_ref, t_ref):
    h = h_ref[...]
    a = _modulate(h, g_ref[...], sh_ref[...], sc_ref[...]).astype(BF16)
    tn = 512
    ssq = jnp.zeros((SG_TM, 1), F32)
    for c in range(2 * D // tn):
        z = jax.nn.gelu(_dot(a, win_ref[:, c * tn:(c + 1) * tn]), approximate=True)
        if c * tn < D:
            u_ref[:, c * tn:(c + 1) * tn] = z
        else:
            v_ref[:, c * tn - D:(c + 1) * tn - D] = z
            ssq = ssq + jnp.sum(z * z, axis=-1, keepdims=True)
    scale = lax.rsqrt(ssq * (1.0 / D) + EPS)
    vn = (v_ref[...] * scale * gv_ref[...]).astype(BF16)
    for ch in range(SG_TM // SG_CHUNK):
        r0 = ch * SG_CHUNK
        for grp in range(SG_GROUPS):
            c0 = grp * SG_GD
            mixed = _dot(ws_ref[grp], vn[r0:r0 + SG_CHUNK, c0:c0 + SG_GD])
            bs = bs_ref[grp]
            mixed = mixed + jnp.concatenate([bs, bs], axis=-1)
            t_ref[r0:r0 + SG_CHUNK, c0:c0 + SG_GD] = (u_ref[r0:r0 + SG_CHUNK, c0:c0 + SG_GD] * mixed).astype(BF16)
    o_ref[...] = h + gt_ref[...] * _dot(t_ref[...], wout_ref[...])


def _sg(h, mods, g_norm, w_in, g_v, w_s, b_s, w_out):
    return pl.pallas_call(
        _sg_kernel,
        out_shape=jax.ShapeDtypeStruct((T, D), F32),
        grid=(T // SG_TM,),
        in_specs=[
            pl.BlockSpec((SG_TM, D), lambda i: (i, 0)),
            _const_spec((1, D)),
            _mod_spec(0, SG_TM), _mod_spec(1, SG_TM), _mod_spec(2, SG_TM),
            _const_spec((D, 2 * D)),
            _const_spec((1, D)),
            _const_spec((SG_GROUPS, SG_CHUNK, SG_CHUNK)),
            _const_spec((SG_GROUPS, SG_CHUNK, LANES)),
            _const_spec((D, D)),
        ],
        out_specs=pl.BlockSpec((SG_TM, D), lambda i: (i, 0)),
        scratch_shapes=[pltpu.VMEM((SG_TM, D), F32), pltpu.VMEM((SG_TM, D), F32), pltpu.VMEM((SG_TM, D), BF16)],
        compiler_params=_cp(("arbitrary",)),
        name="spatial_gating",
    )(h, g_norm, mods, mods, mods, w_in, g_v, w_s, b_s, w_out)


def _gqa_qkv_kernel(h_ref, g_ref, sh_ref, sc_ref, w_ref, gq_ref, gk_ref, cos_ref, sin_ref,
                    q_ref, k_ref, v_ref):
    a = _modulate(h_ref[...], g_ref[...], sh_ref[...], sc_ref[...]).astype(BF16)
    cos = cos_ref[...]
    sin = sin_ref[...]
    lane = lax.broadcasted_iota(jnp.int32, (1, LANES), 1)
    head_lanes = lane < ATT_DH
    first_half = (lane % ATT_DH) < (ATT_DH // 2)
    tn = 256
    ri = lax.broadcasted_iota(jnp.int32, (tn, tn), 0) // ATT_DH
    ci = lax.broadcasted_iota(jnp.int32, (tn, tn), 1) // ATT_DH
    head_mean = jnp.where(ri == ci, 1.0 / ATT_DH, 0.0).astype(BF16)
    n_qk = (ATT_HEADS + ATT_KV) * ATT_DH
    n_q = ATT_HEADS * ATT_DH
    for c in range(w_ref.shape[1] // tn):
        y2 = _dot(a, w_ref[:, c * tn:(c + 1) * tn])
        if c * tn < n_qk:
            ms2 = _dot((y2 * y2).astype(BF16), head_mean)
        for s in range(tn // LANES):
            col = c * tn + s * LANES
            y = y2[:, s * LANES:(s + 1) * LANES]
            if col < n_qk:
                gain = gq_ref[...] if col < n_q else gk_ref[...]
                yn = y * lax.rsqrt(ms2[:, s * LANES:(s + 1) * LANES] + EPS) * gain
                partner = jnp.where(first_half, pltpu.roll(yn, LANES - ATT_DH // 2, 1), pltpu.roll(yn, ATT_DH // 2, 1))
                y = yn * cos + partner * sin
            if col < n_q:
                dst, base, fill = q_ref, col, 0.0
            elif col < n_qk:
                dst, base, fill = k_ref, col - n_q, 0.0
            else:
                dst, base, fill = v_ref, col - n_qk, 1.0
            dst[:, 2 * base:2 * base + LANES] = jnp.where(head_lanes, y, fill).astype(BF16)
            dst[:, 2 * base + LANES:2 * base + 2 * LANES] = jnp.where(
                head_lanes, pltpu.roll(y, ATT_DH, 1), fill).astype(BF16)


def _gqa_qkv(h, mods, g_norm, w, gq, gk, cos_t, sin_t):
    tab_spec = pl.BlockSpec((TM, LANES), lambda i: (jnp.where(i * TM >= T_LAT, L // TM, i % (L // TM)), 0))
    nq, nkv = ATT_HEADS * LANES, ATT_KV * LANES
    return pl.pallas_call(
        _gqa_qkv_kernel,
        out_shape=(jax.ShapeDtypeStruct((T, nq), BF16), jax.ShapeDtypeStruct((T, nkv), BF16),
                   jax.ShapeDtypeStruct((T, nkv), BF16)),
        grid=(T // TM,),
        in_specs=[
            pl.BlockSpec((TM, D), lambda i: (i, 0)),
            _const_spec((1, D)),
            _mod_spec(0, TM), _mod_spec(1, TM),
            _const_spec(w.shape),
            _const_spec((1, LANES)), _const_spec((1, LANES)),
            tab_spec, tab_spec,
        ],
        out_specs=(pl.BlockSpec((TM, nq), lambda i: (i, 0)), pl.BlockSpec((TM, nkv), lambda i: (i, 0)),
                   pl.BlockSpec((TM, nkv), lambda i: (i, 0))),
        compiler_params=_cp(("arbitrary",)),
        name="gqa_qkv_rope",
    )(h, g_norm, mods, mods, w, gq, gk, cos_t, sin_t)


def _gqa_kernel(q_ref, k_ref, v_ref, kc_ref, vc_ref, o_ref):
    lane = lax.broadcasted_iota(jnp.int32, (1, LANES), 1)
    nh = ATT_HEADS // ATT_KV
    qs = jnp.concatenate([q_ref[:, i * LANES:(i + 1) * LANES] for i in range(nh)], axis=0)
    chunks = [(k_ref, v_ref, c * GQ_CK, GQ_CK) for c in range(L // GQ_CK)] + [(kc_ref, vc_ref, 0, NCTX)]
    m = jnp.full((nh * GQ_TQ, 1), -jnp.inf, F32)
    acc = jnp.zeros((nh * GQ_TQ, LANES), F32)
    for kr, vr, c0, cn in chunks:
        s = _dot_nt(qs, kr[c0:c0 + cn, :])
        m_new = jnp.maximum(m, jnp.max(s, axis=-1, keepdims=True))
        p = jnp.exp2(s - m_new).astype(BF16)
        acc = jnp.exp2(m - m_new) * acc + _dot(p, vr[c0:c0 + cn, :])
        m = m_new
    o = acc * (1.0 / acc[:, ATT_DH:ATT_DH + 1])
    for j in range(nh // 2):
        oa = o[2 * j * GQ_TQ:(2 * j + 1) * GQ_TQ]
        ob = o[(2 * j + 1) * GQ_TQ:(2 * j + 2) * GQ_TQ]
        o_ref[:, j * LANES:(j + 1) * LANES] = jnp.where(lane < ATT_DH, oa, pltpu.roll(ob, ATT_DH, 1)).astype(BF16)


def _gqa_attend(q, k, v):
    nh = ATT_HEADS // ATT_KV
    nqt = L // GQ_TQ
    cb = T_LAT // NCTX
    return pl.pallas_call(
        _gqa_kernel,
        out_shape=jax.ShapeDtypeStruct((T_LAT, D), BF16),
        grid=(B, ATT_KV, nqt),
        in_specs=[
            pl.BlockSpec((GQ_TQ, nh * LANES), lambda b, h, t: (b * nqt + t, h)),
            pl.BlockSpec((L, LANES), lambda b, h, t: (b, h)),
            pl.BlockSpec((L, LANES), lambda b, h, t: (b, h)),
            pl.BlockSpec((NCTX, LANES), lambda b, h, t: (cb + b, h)),
            pl.BlockSpec((NCTX, LANES), lambda b, h, t: (cb + b, h)),
        ],
        out_specs=pl.BlockSpec((GQ_TQ, nh * ATT_DH), lambda b, h, t: (b * nqt + t, h)),
        compiler_params=_cp(("arbitrary", "arbitrary", "arbitrary")),
        name="gqa_attention",
    )(q, k, v, k, v)


def _rope_tables():
    t = jnp.arange(L)
    row = (t // GRID_W).astype(F32)
    col = (t % GRID_W).astype(F32)
    n_freq = ATT_DH // 4
    inv_freq = ROPE_THETA ** (-jnp.arange(n_freq, dtype=F32) / n_freq)
    ang = jnp.concatenate([row[:, None] * inv_freq, col[:, None] * inv_freq], axis=-1)
    cos, sin = jnp.cos(ang), jnp.sin(ang)
    cos_t = jnp.concatenate([cos, cos, cos, cos], axis=-1)
    sin_t = jnp.concatenate([-sin, sin, -sin, sin], axis=-1)
    cos_t = jnp.concatenate([cos_t, jnp.ones((TM, LANES), F32)], axis=0)
    sin_t = jnp.concatenate([sin_t, jnp.zeros((TM, LANES), F32)], axis=0)
    return cos_t, sin_t


def kernel(x, c, ctx, c_ctx, w_mod, b_mod, g_norm_mix, g_norm_ffn, w_ffn_up, w_ffn_conv, b_ffn_conv, w_ffn_down,
           w_fnet_out, w_na_qkv, na_rel_bias, w_na_out, w_sg_in, g_sg_v, w_sg_spatial, b_sg_spatial, w_sg_out,
           w_att_qkv, g_att_q, g_att_k, w_att_out, g_final):
    assert x.shape == (B, L, D) and ctx.shape == (B, NCTX, D) and w_mod.shape[0] == DEPTH == 4

    h = jnp.concatenate([x.reshape(T_LAT, D), ctx.reshape(T_CTX, D)], axis=0)
    cond8 = jnp.concatenate([c, c_ctx[None, :], jnp.zeros((8 - B - 1, D), F32)], axis=0)
    mods_all = _ada(cond8, w_mod, b_mod)

    def layer_mods(i):
        return mods_all[i, :B + 1].reshape((B + 1) * N_MOD, 1, D)

    def ffn(h, i, *, last):
        args = (layer_mods(i), g_norm_ffn[i][None, :], w_ffn_up[i].astype(BF16), w_ffn_conv[i], b_ffn_conv[i][None, :],
                w_ffn_down[i].astype(BF16), g_final[None, :])
        nl = T_LAT // TM
        if last:
            return _ffn(h, *args, tm=TM, row0=0, n_compute=nl, n_blocks=nl, out_rows=T_LAT, in_place=False,
                        final_norm=True)
        h = _ffn(h, *args, tm=NCTX, row0=T_LAT, n_compute=B, n_blocks=B, out_rows=T, in_place=True, final_norm=False)
        return _ffn(h, *args, tm=TM, row0=0, n_compute=nl, n_blocks=T // TM, out_rows=T, in_place=False,
                    final_norm=False)

    mods = layer_mods(0)
    ang_c = _dft_angles(FNET_GD)
    fc = jnp.arange(FNET_GD, dtype=jnp.int32)
    a_c = ((fc[:, None] * fc[None, :]) % FNET_GD).astype(F32) * (2.0 * np.pi / FNET_GD)
    cs_chan = jnp.concatenate([jnp.cos(a_c), jnp.sin(a_c)], axis=1).astype(BF16)
    pq = _fnet_chan(h, mods, g_norm_mix[0][None, :], cs_chan)
    w_f = w_fnet_out[0].astype(BF16)
    h = _fnet_seq(h, pq, _dft_angles(L), mods, w_f, seq=L, row0=0, mod_row_fn=lambda b: b, name="fnet_seq_lat")
    h = _fnet_seq(h, pq, ang_c, mods, w_f, seq=NCTX, row0=T_LAT, mod_row_fn=lambda b: B, name="fnet_seq_ctx")
    h = ffn(h, 0, last=False)

    mods = layer_mods(1)
    scale = NA_DH ** -0.5 * LOG2E
    w_qkv = jnp.concatenate([w_na_qkv[0][:, :D] * scale, w_na_qkv[0][:, D:]], axis=1).astype(BF16)
    qkv = _modmm(h, mods, g_norm_mix[1][None, :], w_qkv, name="na_qkv")
    o_lat = _na_attend(qkv, _na_bias_table(na_rel_bias[0]))
    o_ctx = _na_ctx_attend(qkv)
    h = _outproj(h, o_lat, o_ctx, mods, w_na_out[0].astype(BF16), n_blocks=T // TM, name="na_out")
    h = ffn(h, 1, last=False)

    mods = layer_mods(2)
    b_s = jnp.broadcast_to(b_sg_spatial[0][:, :, None], (SG_GROUPS, SG_CHUNK, LANES))
    h = _sg(h, mods, g_norm_mix[2][None, :], w_sg_in[0].astype(BF16), g_sg_v[0][None, :],
            w_sg_spatial[0].astype(BF16), b_s, w_sg_out[0].astype(BF16))
    h = ffn(h, 2, last=False)

    mods = layer_mods(3)
    gq = jnp.tile(g_att_q[0] * (ATT_DH ** -0.5 * LOG2E), LANES // ATT_DH)[None, :]
    gk = jnp.tile(g_att_k[0], LANES // ATT_DH)[None, :]
    cos_t, sin_t = _rope_tables()
    q, k, v = _gqa_qkv(h, mods, g_norm_mix[3][None, :], w_att_qkv[0].astype(BF16), gq, gk, cos_t, sin_t)
    o = _gqa_attend(q, k, v)
    h = _outproj(h, o, o, mods, w_att_out[0].astype(BF16), n_blocks=T_LAT // TM, name="gqa_out")
    out = ffn(h, 3, last=True)
    return out.reshape(B, L, D)
```

```python
import functools

import numpy as np
import jax
import jax.numpy as jnp
from jax import lax
from jax.experimental import pallas as pl
from jax.experimental.pallas import tpu as pltpu

F32 = jnp.float32
BF16 = jnp.bfloat16

D = 1024
B = 4
L = 4096
NCTX = 256
DEPTH = 4
GRID_W = 64
EPS = 1e-6
N_MOD = 6
T_LAT = B * L
T_CTX = B * NCTX
T = T_LAT + T_CTX

FNET_GROUPS = 4
FNET_GD = D // FNET_GROUPS

NA_HEADS = 16
NA_DH = D // NA_HEADS
NA_WIN_ROWS = 8
NA_WIN_COLS = 16
NA_QROWS = 4
NA_KROWS = 12
NA_TQ = NA_QROWS * GRID_W
NA_TK = NA_KROWS * GRID_W
NA_HB = 4

SG_CHUNK = 128
SG_GROUPS = 4
SG_GD = D // SG_GROUPS

ATT_HEADS = 16
ATT_KV = 4
ATT_DH = 64
ROPE_THETA = 10000.0
GQ_TQ = 256
GQ_CK = 256

FFN_DIM = 2816
FFN_TF = 256
FFN_NF = FFN_DIM // FFN_TF
HALO = 16

TM = 1024
SG_TM = 512
LANES = 128
NEG = -1e30
LOG2E = float(np.log2(np.e))

VMEM_LIMIT = 56 * 1024 * 1024


def _cp(sem, vmem=VMEM_LIMIT):
    return pltpu.CompilerParams(dimension_semantics=sem, vmem_limit_bytes=vmem)


def _mod_row(i, tm):
    return (i * tm) // L


def _mod_spec(k, tm, b0=0):
    return pl.BlockSpec((None, 1, D), lambda i: (_mod_row(b0 + i, tm) * N_MOD + k, 0, 0))


def _const_spec(shape, single_buffer=False):
    nd = len(shape)
    if single_buffer:
        return pl.BlockSpec(shape, lambda *_: (0,) * nd, pipeline_mode=pl.Buffered(1))
    return pl.BlockSpec(shape, lambda *_: (0,) * nd)


def _layer_spec(shape, layer, single_buffer=False):
    nd = len(shape)
    mode = dict(pipeline_mode=pl.Buffered(1)) if single_buffer else {}
    return pl.BlockSpec((None,) + tuple(shape), lambda *_: (layer,) + (0,) * nd, **mode)


def _modulate(x, g, shift, scale):
    ms = jnp.mean(x * x, axis=-1, keepdims=True)
    return x * lax.rsqrt(ms + EPS) * (g * (1.0 + scale)) + shift


def _dot(a, b):
    return jnp.dot(a, b, preferred_element_type=F32)


def _dot_nt(a, b):
    return lax.dot_general(a, b, (((1,), (1,)), ((), ())), preferred_element_type=F32)


def _ada_kernel(c_ref, w_ref, b_ref, o_ref):
    c = c_ref[...]
    s = (c * jax.nn.sigmoid(c)).astype(BF16)
    o_ref[0] = _dot(s, w_ref[0].astype(BF16)) + b_ref[0]


def _ada(cond8, w_mod, b_mod):
    tn = 2048
    n = N_MOD * D
    return pl.pallas_call(
        _ada_kernel,
        out_shape=jax.ShapeDtypeStruct((DEPTH, 8, n), F32),
        grid=(DEPTH, n // tn),
        in_specs=[
            pl.BlockSpec((8, D), lambda l, j: (0, 0)),
            pl.BlockSpec((1, D, tn), lambda l, j: (l, 0, j)),
            pl.BlockSpec((1, 1, tn), lambda l, j: (l, 0, j)),
        ],
        out_specs=pl.BlockSpec((1, 8, tn), lambda l, j: (l, 0, j)),
        compiler_params=_cp(("arbitrary", "arbitrary")),
        name="ada_params",
    )(cond8, w_mod, b_mod.reshape(DEPTH, 1, n))


def _ffn_kernel(hp_ref, h_ref, hn_ref, g_ref, sh_ref, sc_ref, gt_ref, wu_ref, wc_ref, bc_ref,
                wd_ref, gf_ref, o_ref, a_ref, gb_ref, vb_ref, *, tm, blocks_per_seq, n_compute, final_norm):
    i = pl.program_id(0)

    @pl.when(i >= n_compute)
    def _():
        o_ref[...] = h_ref[...]

    @pl.when(i < n_compute)
    def _():
        _ffn_block(hp_ref, h_ref, hn_ref, g_ref, sh_ref, sc_ref, gt_ref, wu_ref, wc_ref, bc_ref, wd_ref,
                   gf_ref, o_ref, a_ref, gb_ref, vb_ref, tm=tm, blocks_per_seq=blocks_per_seq, final_norm=final_norm)


def _ffn_block(hp_ref, h_ref, hn_ref, g_ref, sh_ref, sc_ref, gt_ref, wu_ref, wc_ref, bc_ref, wd_ref,
               gf_ref, o_ref, a_ref, gb_ref, vb_ref, *, tm, blocks_per_seq, final_norm):
    i = pl.program_id(0)
    g = g_ref[...]
    sh = sh_ref[...]
    sc = sc_ref[...]

    if blocks_per_seq == 1:
        a_ref[0:HALO, :] = jnp.zeros((HALO, D), BF16)
        a_ref[HALO + tm:, :] = jnp.zeros((HALO, D), BF16)
    else:
        seq_first = (i % blocks_per_seq) == 0
        seq_last = (i % blocks_per_seq) == blocks_per_seq - 1
        a_ref[0:HALO, :] = jnp.where(seq_first, 0.0, _modulate(hp_ref[...], g, sh, sc)).astype(BF16)
        a_ref[HALO + tm:, :] = jnp.where(seq_last, 0.0, _modulate(hn_ref[...], g, sh, sc)).astype(BF16)

    def up(j, r0=0, r1=tm):
        c0 = j * FFN_TF
        e0 = 0 if r0 == 0 else HALO + r0
        e1 = tm + 2 * HALO if r1 == tm else HALO + r1
        gb_ref[j % 2, e0:e1, :] = _dot(a_ref[e0:e1, :], wu_ref[:, c0:c0 + FFN_TF])
        vb_ref[j % 2, r0:r1, :] = _dot(a_ref[HALO + r0:HALO + r1, :],
                                       wu_ref[:, FFN_DIM + c0:FFN_DIM + c0 + FFN_TF])

    n_piece = 4 if tm == TM else 1
    pieces = [(p * tm // n_piece, (p + 1) * tm // n_piece) for p in range(n_piece)]
    for r0, r1 in pieces:
        a_ref[HALO + r0:HALO + r1, :] = _modulate(h_ref[r0:r1, :], g, sh, sc).astype(BF16)
        up(0, r0, r1)
    for j in range(FFN_NF):
        if j + 1 < FFN_NF:
            up(j + 1)
        gb = gb_ref.at[j % 2]
        c0 = j * FFN_TF
        wc = wc_ref[:, c0:c0 + FFN_TF]
        gc = (gb[HALO:HALO + tm, :] * wc[1:2]
              + gb[HALO - 1:HALO - 1 + tm, :] * wc[0:1]
              + gb[HALO + 1:HALO + 1 + tm, :] * wc[2:3]
              + bc_ref[:, c0:c0 + FFN_TF])
        u = (gc * jax.nn.sigmoid(gc) * vb_ref[j % 2]).astype(BF16)
        if j + 1 < FFN_NF:
            y = _dot(u, wd_ref[c0:c0 + FFN_TF, :])
            if j == 0:
                o_ref[...] = y
            else:
                o_ref[...] += y
        else:
            for r0, r1 in pieces:
                acc = o_ref[r0:r1, :] + _dot(u[r0:r1, :], wd_ref[c0:c0 + FFN_TF, :])
                out = h_ref[r0:r1, :] + gt_ref[...] * acc
                if final_norm:
                    ms = jnp.mean(out * out, axis=-1, keepdims=True)
                    out = out * lax.rsqrt(ms + EPS) * gf_ref[...]
                o_ref[r0:r1, :] = out


def _ffn(h, mods, g_norm, wu, wc, bc, wd, g_final, *, layer, tm, row0, n_compute, n_blocks, out_rows, in_place,
         final_norm):
    hb = tm // HALO
    b0 = row0 // tm
    last_halo = T // HALO - 1
    blocks_per_seq = L // tm if row0 == 0 else NCTX // tm
    assert not (in_place and blocks_per_seq != 1)
    kern = functools.partial(_ffn_kernel, tm=tm, blocks_per_seq=blocks_per_seq, n_compute=n_compute,
                             final_norm=final_norm)
    if blocks_per_seq == 1:
        halo = jnp.zeros((HALO, D), F32)
        halo_specs = [_const_spec((HALO, D)), _const_spec((HALO, D))]
    else:
        halo = h
        halo_specs = [pl.BlockSpec((HALO, D), lambda i: (jnp.maximum((b0 + i) * hb - 1, 0), 0)),
                      pl.BlockSpec((HALO, D), lambda i: (jnp.minimum((b0 + i + 1) * hb, last_halo), 0))]
    in_specs = [
        halo_specs[0],
        pl.BlockSpec((tm, D), lambda i: (b0 + i, 0)),
        halo_specs[1],
        _const_spec((1, D)),
        _mod_spec(3, tm, b0), _mod_spec(4, tm, b0), _mod_spec(5, tm, b0),
        _layer_spec((D, 2 * FFN_DIM), layer, single_buffer=True),
        _layer_spec((3, FFN_DIM), layer),
        _layer_spec((1, FFN_DIM), layer),
        _layer_spec((FFN_DIM, D), layer, single_buffer=True),
        _const_spec((1, D)),
    ]
    return pl.pallas_call(
        kern,
        out_shape=jax.ShapeDtypeStruct((out_rows, D), F32),
        grid=(n_blocks,),
        in_specs=in_specs,
        out_specs=pl.BlockSpec((tm, D), lambda i: (b0 + i, 0)),
        scratch_shapes=[pltpu.VMEM((tm + 2 * HALO, D), BF16), pltpu.VMEM((2, tm + 2 * HALO, FFN_TF), F32),
                        pltpu.VMEM((2, tm, FFN_TF), F32)],
        input_output_aliases={1: 0} if in_place else {},
        compiler_params=_cp(("arbitrary",)),
        name="conv_ffn_lat" if row0 == 0 else "conv_ffn_ctx",
    )(halo, h, halo, g_norm, mods, mods, mods, wu, wc, bc, wd, g_final)


def _modmm_kernel(h_ref, g_ref, sh_ref, sc_ref, w_ref, o_ref, *, tn):
    a = _modulate(h_ref[...], g_ref[...], sh_ref[...], sc_ref[...]).astype(BF16)
    for c in range(w_ref.shape[1] // tn):
        o_ref[:, c * tn:(c + 1) * tn] = _dot(a, w_ref[:, c * tn:(c + 1) * tn]).astype(BF16)


def _modmm(h, mods, g_norm, w, *, name):
    n = w.shape[1]
    return pl.pallas_call(
        functools.partial(_modmm_kernel, tn=512),
        out_shape=jax.ShapeDtypeStruct((T, n), BF16),
        grid=(T // TM,),
        in_specs=[
            pl.BlockSpec((TM, D), lambda i: (i, 0)),
            _const_spec((1, D)),
            _mod_spec(0, TM), _mod_spec(1, TM),
            _const_spec((D, n)),
        ],
        out_specs=pl.BlockSpec((TM, n), lambda i: (i, 0)),
        compiler_params=_cp(("arbitrary",)),
        name=name,
    )(h, g_norm, mods, mods, w)


def _outproj_kernel(h_ref, xl_ref, xc_ref, gt_ref, w_ref, o_ref, *, n_lat):
    i = pl.program_id(0)

    @pl.when(i < n_lat)
    def _():
        o_ref[...] = h_ref[...] + gt_ref[...] * _dot(xl_ref[...], w_ref[...])

    @pl.when(i >= n_lat)
    def _():
        o_ref[...] = h_ref[...] + gt_ref[...] * _dot(xc_ref[...], w_ref[...])


def _outproj(h, x_lat, x_ctx, mods, w, *, n_blocks, name):
    n_lat = T_LAT // TM
    return pl.pallas_call(
        functools.partial(_outproj_kernel, n_lat=n_lat),
        out_shape=jax.ShapeDtypeStruct((T, D), F32),
        grid=(n_blocks,),
        in_specs=[
            pl.BlockSpec((TM, D), lambda i: (i, 0)),
            pl.BlockSpec((TM, D), lambda i: (jnp.minimum(i, n_lat - 1), 0)),
            pl.BlockSpec((TM, D), lambda i: (jnp.maximum(i - n_lat, 0), 0)),
            _mod_spec(2, TM),
            _const_spec((D, D)),
        ],
        out_specs=pl.BlockSpec((TM, D), lambda i: (i, 0)),
        input_output_aliases={0: 0},
        compiler_params=_cp(("arbitrary",)),
        name=name,
    )(h, x_lat, x_ctx, mods, w)


def _fnet_chan_kernel(h_ref, g_ref, sh_ref, sc_ref, cs_ref, o_ref):
    a = _modulate(h_ref[...], g_ref[...], sh_ref[...], sc_ref[...]).astype(BF16)
    cs = cs_ref[...]
    for grp in range(FNET_GROUPS):
        pq = _dot(a[:, grp * FNET_GD:(grp + 1) * FNET_GD], cs)
        o_ref[0, :, grp * FNET_GD:(grp + 1) * FNET_GD] = pq[:, :FNET_GD].astype(BF16)
        o_ref[1, :, grp * FNET_GD:(grp + 1) * FNET_GD] = pq[:, FNET_GD:].astype(BF16)


def _fnet_chan(h, mods, g_norm, cs_chan):
    return pl.pallas_call(
        _fnet_chan_kernel,
        out_shape=jax.ShapeDtypeStruct((2, T, D), BF16),
        grid=(T // TM,),
        in_specs=[
            pl.BlockSpec((TM, D), lambda i: (i, 0)),
            _const_spec((1, D)),
            _mod_spec(0, TM), _mod_spec(1, TM),
            _const_spec((FNET_GD, 2 * FNET_GD)),
        ],
        out_specs=pl.BlockSpec((2, TM, D), lambda i: (0, i, 0)),
        compiler_params=_cp(("arbitrary",)),
        name="fnet_channel_dft",
    )(h, g_norm, mods, mods, cs_chan)


def _fnet_seq_kernel(hi_ref, lo_ref, pq_ref, h_ref, gt_ref, w_ref, o_ref, acc_ref, tab_ref, *, r, nkk, inv_norm):
    k = pl.program_id(2)
    use_cos = k < nkk
    c1, s1 = hi_ref[0], hi_ref[1]
    c0, s0 = lo_ref[0], lo_ref[1]
    a1 = jnp.where(use_cos, c1, -s1)
    b1 = jnp.where(use_cos, -s1, -c1)
    for fh in range(c1.shape[0]):
        tab_ref[fh * r:(fh + 1) * r, :] = (a1[fh:fh + 1, :] * c0 + b1[fh:fh + 1, :] * s0).astype(BF16)
    y = _dot(tab_ref[...], pq_ref[0])

    @pl.when(k == 0)
    def _():
        acc_ref[...] = y

    @pl.when((k > 0) & (k < 2 * nkk - 1))
    def _():
        acc_ref[...] += y

    @pl.when(k == 2 * nkk - 1)
    def _():
        f = ((acc_ref[...] + y) * inv_norm).astype(BF16)
        o_ref[...] = h_ref[...] + gt_ref[...] * _dot(f, w_ref[...])


def _fnet_seq(h, pq, angles, mods, w, *, seq, row0, mod_row_fn, name):
    hi, lo = angles
    r = lo.shape[1]
    tm = min(TM, seq)
    tk = min(2048, seq)
    nkk = seq // tk
    mt = seq // tm
    rb0 = row0 // tm
    kb0 = row0 // tk
    kern = functools.partial(_fnet_seq_kernel, r=r, nkk=nkk, inv_norm=1.0 / np.sqrt(seq * FNET_GD))
    return pl.pallas_call(
        kern,
        out_shape=jax.ShapeDtypeStruct((T, D), F32),
        grid=(B, mt, 2 * nkk),
        in_specs=[
            pl.BlockSpec((2, tm // r, tk), lambda b, m, k: (0, m, k % nkk)),
            pl.BlockSpec((2, r, tk), lambda b, m, k: (0, 0, k % nkk)),
            pl.BlockSpec((1, tk, D), lambda b, m, k: (k // nkk, kb0 + b * nkk + k % nkk, 0)),
            pl.BlockSpec((tm, D), lambda b, m, k: (rb0 + b * mt + m, 0)),
            pl.BlockSpec((None, 1, D), lambda b, m, k: (mod_row_fn(b) * N_MOD + 2, 0, 0)),
            _const_spec((D, D)),
        ],
        out_specs=pl.BlockSpec((tm, D), lambda b, m, k: (rb0 + b * mt + m, 0)),
        scratch_shapes=[pltpu.VMEM((tm, D), F32), pltpu.VMEM((tm, tk), BF16)],
        input_output_aliases={3: 0},
        compiler_params=_cp(("arbitrary", "arbitrary", "arbitrary")),
        name=name,
    )(hi, lo, pq, h, mods, w)


def _dft_angles(n):
    r = int(np.sqrt(n))
    assert r * r == n
    t = jnp.arange(n, dtype=jnp.int32)[None, :]
    f = jnp.arange(r, dtype=jnp.int32)[:, None]
    a_hi = ((f * t) % r).astype(F32) * (2.0 * np.pi / r)
    a_lo = ((f * t) % n).astype(F32) * (2.0 * np.pi / n)
    return jnp.stack([jnp.cos(a_hi), jnp.sin(a_hi)]), jnp.stack([jnp.cos(a_lo), jnp.sin(a_lo)])


def _na_bias_table(rpb):
    n_co = 2 * NA_WIN_COLS - 1
    qc = np.arange(GRID_W)[:, None]
    kc = np.arange(GRID_W)[None, :]
    cs = np.clip(qc - NA_WIN_COLS // 2, 0, GRID_W - NA_WIN_COLS)
    col_ok = (kc >= cs) & (kc < cs + NA_WIN_COLS)
    col_sel = (kc - qc + NA_WIN_COLS - 1)[None] == np.arange(n_co)[:, None, None]
    by_col = jnp.einsum("hrd,dqk->hrqk", rpb * LOG2E, jnp.asarray(col_sel, F32), precision=lax.Precision.HIGHEST)
    a = jnp.where(col_ok, by_col, NEG)
    z = jnp.zeros((NA_HEADS, 1, GRID_W, GRID_W), F32)
    return jnp.concatenate([jnp.concatenate([z, a], axis=1), jnp.concatenate([a, z], axis=1)], axis=-1)


def _na_row_tables():
    rows = L // GRID_W
    nq = L // NA_TQ
    npair = NA_KROWS // 2
    idx = np.zeros((nq, NA_QROWS * npair), np.int32)
    rmask = np.zeros((nq, NA_QROWS * npair, LANES), np.float32)
    for g in range(nq):
        k0 = int(np.clip(NA_QROWS * g - NA_WIN_ROWS // 2, 0, rows - NA_KROWS))
        for qi in range(NA_QROWS):
            r = NA_QROWS * g + qi
            rs = int(np.clip(r - NA_WIN_ROWS // 2, 0, rows - NA_WIN_ROWS))
            for m in range(npair):
                kr = k0 + 2 * m
                idx[g, qi * npair + m] = int(np.clip(kr - r + NA_WIN_ROWS, 0, 2 * NA_WIN_ROWS - 1))
                for half in range(2):
                    if not rs <= kr + half < rs + NA_WIN_ROWS:
                        rmask[g, qi * npair + m, half * NA_DH:(half + 1) * NA_DH] = NEG
    return idx, rmask


def _na_kernel(idx_ref, q_ref, k_ref, v_ref, kc_ref, vc_ref, tab_ref, rmask_ref, o_ref, snb_ref, scx_ref):
    g = pl.program_id(2)
    k0 = jnp.clip(NA_QROWS * g - NA_WIN_ROWS // 2, 0, L // GRID_W - NA_KROWS)
    start = pl.multiple_of(k0 * GRID_W, GRID_W)
    lane = lax.broadcasted_iota(jnp.int32, (1, LANES), 1)
    lower = lane < NA_DH
    npair = NA_KROWS // 2

    heads = [(t, a) for t in range(NA_HB) for a in range(2)]

    def scores(i):
        t, a = heads[i]
        cols = slice(t * LANES, (t + 1) * LANES)
        q = q_ref[:, cols]
        qm = jnp.where((lane // NA_DH) == a, q, jnp.zeros_like(q))
        bias = jnp.concatenate(
            [jnp.concatenate([tab_ref[2 * t + a, idx_ref[g, qi * npair + m]]
                              + rmask_ref[qi * npair + m:qi * npair + m + 1, :] for m in range(npair)], axis=1)
             for qi in range(NA_QROWS)], axis=0)
        snb_ref[i % 2] = _dot_nt(qm, k_ref[pl.ds(start, NA_TK), cols]) + bias
        scx_ref[i % 2] = _dot_nt(qm, kc_ref[:, cols])

    scores(0)
    prev = None
    for i, (t, a) in enumerate(heads):
        if i + 1 < len(heads):
            scores(i + 1)
        cols = slice(t * LANES, (t + 1) * LANES)
        sel = (lane // NA_DH) == a
        s_nb = snb_ref[i % 2]
        s_cx = scx_ref[i % 2]
        mx = jnp.maximum(jnp.max(s_nb, axis=-1, keepdims=True), jnp.max(s_cx, axis=-1, keepdims=True))
        vw = v_ref[pl.ds(start, NA_TK), cols]
        vc = vc_ref[:, cols]
        o = (_dot(jnp.exp2(s_nb - mx).astype(BF16), jnp.where(sel, vw, jnp.ones_like(vw)))
             + _dot(jnp.exp2(s_cx - mx).astype(BF16), jnp.where(sel, vc, jnp.ones_like(vc))))
        o = o * (1.0 / o[:, (1 - a) * NA_DH:(1 - a) * NA_DH + 1])
        if a == 0:
            prev = o
        else:
            o_ref[:, cols] = jnp.where(lower, prev, o).astype(BF16)


def _na_attend(qkv, table):
    idx, rmask = _na_row_tables()
    wb = NA_HB * LANES
    nh = D // wb
    nq = L // NA_TQ
    return pl.pallas_call(
        _na_kernel,
        out_shape=jax.ShapeDtypeStruct((T_LAT, D), BF16),
        grid=(B, nh, nq),
        in_specs=[
            pl.BlockSpec(memory_space=pltpu.SMEM),
            pl.BlockSpec((NA_TQ, wb), lambda b, h, g: (b * nq + g, h)),
            pl.BlockSpec((L, wb), lambda b, h, g: (b, nh + h)),
            pl.BlockSpec((L, wb), lambda b, h, g: (b, 2 * nh + h)),
            pl.BlockSpec((NCTX, wb), lambda b, h, g: (T_LAT // NCTX + b, nh + h)),
            pl.BlockSpec((NCTX, wb), lambda b, h, g: (T_LAT // NCTX + b, 2 * nh + h)),
            pl.BlockSpec((2 * NA_HB, 2 * NA_WIN_ROWS, GRID_W, LANES), lambda b, h, g: (h, 0, 0, 0)),
            pl.BlockSpec((None, NA_QROWS * NA_KROWS // 2, LANES), lambda b, h, g: (g, 0, 0)),
        ],
        out_specs=pl.BlockSpec((NA_TQ, wb), lambda b, h, g: (b * nq + g, h)),
        scratch_shapes=[pltpu.VMEM((2, NA_TQ, NA_TK), F32), pltpu.VMEM((2, NA_TQ, NCTX), F32)],
        compiler_params=_cp(("arbitrary", "arbitrary", "arbitrary")),
        name="na_attention",
    )(jnp.asarray(idx), qkv, qkv, qkv, qkv, qkv, table, jnp.asarray(rmask))


def _ctx_attn_kernel(q_ref, kc_ref, vc_ref, o_ref):
    kc = kc_ref[...]
    vc = vc_ref[...]
    q = q_ref[...]
    lane = lax.broadcasted_iota(jnp.int32, (1, LANES), 1)
    outs = []
    for a in range(2):
        qm = jnp.where((lane // NA_DH) == a, q, jnp.zeros_like(q))
        s = _dot_nt(qm, kc)
        p = jnp.exp2(s - jnp.max(s, axis=-1, keepdims=True))
        outs.append(_dot(p.astype(BF16), vc) / jnp.sum(p, axis=-1, keepdims=True))
    o_ref[...] = jnp.where(lane < NA_DH, outs[0], outs[1]).astype(BF16)


def _na_ctx_attend(qkv):
    hp = D // LANES
    cb = T_LAT // NCTX
    return pl.pallas_call(
        _ctx_attn_kernel,
        out_shape=jax.ShapeDtypeStruct((T_CTX, D), BF16),
        grid=(B, hp),
        in_specs=[
            pl.BlockSpec((NCTX, LANES), lambda b, h: (cb + b, h)),
            pl.BlockSpec((NCTX, LANES), lambda b, h: (cb + b, hp + h)),
            pl.BlockSpec((NCTX, LANES), lambda b, h: (cb + b, 2 * hp + h)),
        ],
        out_specs=pl.BlockSpec((NCTX, LANES), lambda b, h: (b, h)),
        compiler_params=_cp(("arbitrary", "arbitrary")),
        name="na_ctx_attention",
    )(qkv, qkv, qkv)


def _sg_kernel(h_ref, g_ref, sh_ref, sc_ref, gt_ref, win_ref, gv_ref, ws_ref, bs_ref, wout_ref, o_ref,
               u_ref, v_ref, t_ref):
    h = h_ref[...]
    a = _modulate(h, g_ref[...], sh_ref[...], sc_ref[...]).astype(BF16)
    tn = 512
    ssq = jnp.zeros((SG_TM, 1), F32)
    for c in range(2 * D // tn):
        z = jax.nn.gelu(_dot(a, win_ref[:, c * tn:(c + 1) * tn]), approximate=True)
        if c * tn < D:
            u_ref[:, c * tn:(c + 1) * tn] = z
        else:
            v_ref[:, c * tn - D:(c + 1) * tn - D] = z
            ssq = ssq + jnp.sum(z * z, axis=-1, keepdims=True)
    scale = lax.rsqrt(ssq * (1.0 / D) + EPS)
    vn = (v_ref[...] * scale * gv_ref[...]).astype(BF16)
    for ch in range(SG_TM // SG_CHUNK):
        r0 = ch * SG_CHUNK
        for grp in range(SG_GROUPS):
            c0 = grp * SG_GD
            mixed = _dot(ws_ref[grp], vn[r0:r0 + SG_CHUNK, c0:c0 + SG_GD])
            bs = bs_ref[grp]
            mixed = mixed + jnp.concatenate([bs, bs], axis=-1)
            t_ref[r0:r0 + SG_CHUNK, c0:c0 + SG_GD] = (u_ref[r0:r0 + SG_CHUNK, c0:c0 + SG_GD] * mixed).astype(BF16)
    o_ref[...] = h + gt_ref[...] * _dot(t_ref[...], wout_ref[...])


def _sg(h, mods, g_norm, w_in, g_v, w_s, b_s, w_out):
    return pl.pallas_call(
        _sg_kernel,
        out_shape=jax.ShapeDtypeStruct((T, D), F32),
        grid=(T // SG_TM,),
        in_specs=[
            pl.BlockSpec((SG_TM, D), lambda i: (i, 0)),
            _const_spec((1, D)),
            _mod_spec(0, SG_TM), _mod_spec(1, SG_TM), _mod_spec(2, SG_TM),
            _const_spec((D, 2 * D)),
            _const_spec((1, D)),
            _const_spec((SG_GROUPS, SG_CHUNK, SG_CHUNK)),
            _const_spec((SG_GROUPS, SG_CHUNK, LANES)),
            _const_spec((D, D)),
        ],
        out_specs=pl.BlockSpec((SG_TM, D), lambda i: (i, 0)),
        scratch_shapes=[pltpu.VMEM((SG_TM, D), F32), pltpu.VMEM((SG_TM, D), F32), pltpu.VMEM((SG_TM, D), BF16)],
        compiler_params=_cp(("arbitrary",)),
        name="spatial_gating",
    )(h, g_norm, mods, mods, mods, w_in, g_v, w_s, b_s, w_out)


def _gqa_qkv_kernel(h_ref, g_ref, sh_ref, sc_ref, w_ref, gq_ref, gk_ref, cos_ref, sin_ref,
                    q_ref, k_ref, v_ref):
    a = _modulate(h_ref[...], g_ref[...], sh_ref[...], sc_ref[...]).astype(BF16)
    cos = cos_ref[...]
    sin = sin_ref[...]
    lane = lax.broadcasted_iota(jnp.int32, (1, LANES), 1)
    head_lanes = lane < ATT_DH
    first_half = (lane % ATT_DH) < (ATT_DH // 2)
    tn = 256
    ri = lax.broadcasted_iota(jnp.int32, (tn, tn), 0) // ATT_DH
    ci = lax.broadcasted_iota(jnp.int32, (tn, tn), 1) // ATT_DH
    head_mean = jnp.where(ri == ci, 1.0 / ATT_DH, 0.0).astype(BF16)
    n_qk = (ATT_HEADS + ATT_KV) * ATT_DH
    n_q = ATT_HEADS * ATT_DH
    for c in range(w_ref.shape[1] // tn):
        y2 = _dot(a, w_ref[:, c * tn:(c + 1) * tn])
        if c * tn < n_qk:
            ms2 = _dot((y2 * y2).astype(BF16), head_mean)
        for s in range(tn // LANES):
            col = c * tn + s * LANES
            y = y2[:, s * LANES:(s + 1) * LANES]
            if col < n_qk:
                gain = gq_ref[...] if col < n_q else gk_ref[...]
                yn = y * lax.rsqrt(ms2[:, s * LANES:(s + 1) * LANES] + EPS) * gain
                partner = jnp.where(first_half, pltpu.roll(yn, LANES - ATT_DH // 2, 1), pltpu.roll(yn, ATT_DH // 2, 1))
                y = yn * cos + partner * sin
            if col < n_q:
                dst, base, fill = q_ref, col, 0.0
            elif col < n_qk:
                dst, base, fill = k_ref, col - n_q, 0.0
            else:
                dst, base, fill = v_ref, col - n_qk, 1.0
            dst[:, 2 * base:2 * base + LANES] = jnp.where(head_lanes, y, fill).astype(BF16)
            dst[:, 2 * base + LANES:2 * base + 2 * LANES] = jnp.where(
                head_lanes, pltpu.roll(y, ATT_DH, 1), fill).astype(BF16)


def _gqa_qkv(h, mods, g_norm, w, gq, gk, cos_t, sin_t):
    tab_spec = pl.BlockSpec((TM, LANES), lambda i: (jnp.where(i * TM >= T_LAT, L // TM, i % (L // TM)), 0))
    nq, nkv = ATT_HEADS * LANES, ATT_KV * LANES
    return pl.pallas_call(
        _gqa_qkv_kernel,
        out_shape=(jax.ShapeDtypeStruct((T, nq), BF16), jax.ShapeDtypeStruct((T, nkv), BF16),
                   jax.ShapeDtypeStruct((T, nkv), BF16)),
        grid=(T // TM,),
        in_specs=[
            pl.BlockSpec((TM, D), lambda i: (i, 0)),
            _const_spec((1, D)),
            _mod_spec(0, TM), _mod_spec(1, TM),
            _const_spec(w.shape),
            _const_spec((1, LANES)), _const_spec((1, LANES)),
            tab_spec, tab_spec,
        ],
        out_specs=(pl.BlockSpec((TM, nq), lambda i: (i, 0)), pl.BlockSpec((TM, nkv), lambda i: (i, 0)),
                   pl.BlockSpec((TM, nkv), lambda i: (i, 0))),
        compiler_params=_cp(("arbitrary",)),
        name="gqa_qkv_rope",
    )(h, g_norm, mods, mods, w, gq, gk, cos_t, sin_t)


def _gqa_kernel(q_ref, k_ref, v_ref, kc_ref, vc_ref, o_ref):
    lane = lax.broadcasted_iota(jnp.int32, (1, LANES), 1)
    nh = ATT_HEADS // ATT_KV
    qs = jnp.concatenate([q_ref[:, i * LANES:(i + 1) * LANES] for i in range(nh)], axis=0)
    chunks = [(k_ref, v_ref, c * GQ_CK, GQ_CK) for c in range(L // GQ_CK)] + [(kc_ref, vc_ref, 0, NCTX)]
    m = jnp.full((nh * GQ_TQ, 1), -jnp.inf, F32)
    acc = jnp.zeros((nh * GQ_TQ, LANES), F32)
    for kr, vr, c0, cn in chunks:
        s = _dot_nt(qs, kr[c0:c0 + cn, :])
        m_new = jnp.maximum(m, jnp.max(s, axis=-1, keepdims=True))
        p = jnp.exp2(s - m_new).astype(BF16)
        acc = jnp.exp2(m - m_new) * acc + _dot(p, vr[c0:c0 + cn, :])
        m = m_new
    o = acc * (1.0 / acc[:, ATT_DH:ATT_DH + 1])
    for j in range(nh // 2):
        oa = o[2 * j * GQ_TQ:(2 * j + 1) * GQ_TQ]
        ob = o[(2 * j + 1) * GQ_TQ:(2 * j + 2) * GQ_TQ]
        o_ref[:, j * LANES:(j + 1) * LANES] = jnp.where(lane < ATT_DH, oa, pltpu.roll(ob, ATT_DH, 1)).astype(BF16)


def _gqa_attend(q, k, v):
    nh = ATT_HEADS // ATT_KV
    nqt = L // GQ_TQ
    cb = T_LAT // NCTX
    return pl.pallas_call(
        _gqa_kernel,
        out_shape=jax.ShapeDtypeStruct((T_LAT, D), BF16),
        grid=(B, ATT_KV, nqt),
        in_specs=[
            pl.BlockSpec((GQ_TQ, nh * LANES), lambda b, h, t: (b * nqt + t, h)),
            pl.BlockSpec((L, LANES), lambda b, h, t: (b, h)),
            pl.BlockSpec((L, LANES), lambda b, h, t: (b, h)),
            pl.BlockSpec((NCTX, LANES), lambda b, h, t: (cb + b, h)),
            pl.BlockSpec((NCTX, LANES), lambda b, h, t: (cb + b, h)),
        ],
        out_specs=pl.BlockSpec((GQ_TQ, nh * ATT_DH), lambda b, h, t: (b * nqt + t, h)),
        compiler_params=_cp(("arbitrary", "arbitrary", "arbitrary")),
        name="gqa_attention",
    )(q, k, v, k, v)


def _rope_tables():
    t = jnp.arange(L)
    row = (t // GRID_W).astype(F32)
    col = (t % GRID_W).astype(F32)
    n_freq = ATT_DH // 4
    inv_freq = ROPE_THETA ** (-jnp.arange(n_freq, dtype=F32) / n_freq)
    ang = jnp.concatenate([row[:, None] * inv_freq, col[:, None] * inv_freq], axis=-1)
    cos, sin = jnp.cos(ang), jnp.sin(ang)
    cos_t = jnp.concatenate([cos, cos, cos, cos], axis=-1)
    sin_t = jnp.concatenate([-sin, sin, -sin, sin], axis=-1)
    cos_t = jnp.concatenate([cos_t, jnp.ones((TM, LANES), F32)], axis=0)
    sin_t = jnp.concatenate([sin_t, jnp.zeros((TM, LANES), F32)], axis=0)
    return cos_t, sin_t


def kernel(x, c, ctx, c_ctx, w_mod, b_mod, g_norm_mix, g_norm_ffn, w_ffn_up, w_ffn_conv, b_ffn_conv, w_ffn_down,
           w_fnet_out, w_na_qkv, na_rel_bias, w_na_out, w_sg_in, g_sg_v, w_sg_spatial, b_sg_spatial, w_sg_out,
           w_att_qkv, g_att_q, g_att_k, w_att_out, g_final):
    assert x.shape == (B, L, D) and ctx.shape == (B, NCTX, D) and w_mod.shape[0] == DEPTH == 4

    h = jnp.concatenate([x.reshape(T_LAT, D), ctx.reshape(T_CTX, D)], axis=0)
    cond8 = jnp.concatenate([c, c_ctx[None, :], jnp.zeros((8 - B - 1, D), F32)], axis=0)
    mods_all = _ada(cond8, w_mod, b_mod)

    w_up_bf = w_ffn_up.astype(BF16)
    w_down_bf = w_ffn_down.astype(BF16)

    def layer_mods(i):
        return mods_all[i, :B + 1].reshape((B + 1) * N_MOD, 1, D)

    def ffn(h, i, *, last):
        args = (layer_mods(i), g_norm_ffn[i][None, :], w_up_bf, w_ffn_conv, b_ffn_conv[:, None, :], w_down_bf,
                g_final[None, :])
        nl = T_LAT // TM
        if last:
            return _ffn(h, *args, layer=i, tm=TM, row0=0, n_compute=nl, n_blocks=nl, out_rows=T_LAT, in_place=False,
                        final_norm=True)
        h = _ffn(h, *args, layer=i, tm=NCTX, row0=T_LAT, n_compute=B, n_blocks=B, out_rows=T, in_place=True, final_norm=False)
        return _ffn(h, *args, layer=i, tm=TM, row0=0, n_compute=nl, n_blocks=T // TM, out_rows=T, in_place=False,
                    final_norm=False)

    mods = layer_mods(0)
    ang_c = _dft_angles(FNET_GD)
    fc = jnp.arange(FNET_GD, dtype=jnp.int32)
    a_c = ((fc[:, None] * fc[None, :]) % FNET_GD).astype(F32) * (2.0 * np.pi / FNET_GD)
    cs_chan = jnp.concatenate([jnp.cos(a_c), jnp.sin(a_c)], axis=1).astype(BF16)
    pq = _fnet_chan(h, mods, g_norm_mix[0][None, :], cs_chan)
    w_f = w_fnet_out[0].astype(BF16)
    h = _fnet_seq(h, pq, _dft_angles(L), mods, w_f, seq=L, row0=0, mod_row_fn=lambda b: b, name="fnet_seq_lat")
    h = _fnet_seq(h, pq, ang_c, mods, w_f, seq=NCTX, row0=T_LAT, mod_row_fn=lambda b: B, name="fnet_seq_ctx")
    h = ffn(h, 0, last=False)

    mods = layer_mods(1)
    scale = NA_DH ** -0.5 * LOG2E
    w_qkv = jnp.concatenate([w_na_qkv[0][:, :D] * scale, w_na_qkv[0][:, D:]], axis=1).astype(BF16)
    qkv = _modmm(h, mods, g_norm_mix[1][None, :], w_qkv, name="na_qkv")
    o_lat = _na_attend(qkv, _na_bias_table(na_rel_bias[0]))
    o_ctx = _na_ctx_attend(qkv)
    h = _outproj(h, o_lat, o_ctx, mods, w_na_out[0].astype(BF16), n_blocks=T // TM, name="na_out")
    h = ffn(h, 1, last=False)

    mods = layer_mods(2)
    b_s = jnp.broadcast_to(b_sg_spatial[0][:, :, None], (SG_GROUPS, SG_CHUNK, LANES))
    h = _sg(h, mods, g_norm_mix[2][None, :], w_sg_in[0].astype(BF16), g_sg_v[0][None, :],
            w_sg_spatial[0].astype(BF16), b_s, w_sg_out[0].astype(BF16))
    h = ffn(h, 2, last=False)

    mods = layer_mods(3)
    gq = jnp.tile(g_att_q[0] * (ATT_DH ** -0.5 * LOG2E), LANES // ATT_DH)[None, :]
    gk = jnp.tile(g_att_k[0], LANES // ATT_DH)[None, :]
    cos_t, sin_t = _rope_tables()
    q, k, v = _gqa_qkv(h, mods, g_norm_mix[3][None, :], w_att_qkv[0].astype(BF16), gq, gk, cos_t, sin_t)
    o = _gqa_attend(q, k, v)
    h = _outproj(h, o, o, mods, w_att_out[0].astype(BF16), n_blocks=T_LAT // TM, name="gqa_out")
    out = ffn(h, 3, last=True)
    return out.reshape(B, L, D)
```

```python
import functools

import numpy as np
import jax
import jax.numpy as jnp
from jax import lax
from jax.experimental import pallas as pl
from jax.experimental.pallas import tpu as pltpu

F32 = jnp.float32
BF16 = jnp.bfloat16

D = 1024
B = 4
L = 4096
NCTX = 256
DEPTH = 4
GRID_W = 64
EPS = 1e-6
N_MOD = 6
T_LAT = B * L
T_CTX = B * NCTX
T = T_LAT + T_CTX

FNET_GROUPS = 4
FNET_GD = D // FNET_GROUPS

NA_HEADS = 16
NA_DH = D // NA_HEADS
NA_WIN_ROWS = 8
NA_WIN_COLS = 16
NA_QROWS = 4
NA_KROWS = 12
NA_TQ = NA_QROWS * GRID_W
NA_TK = NA_KROWS * GRID_W
NA_HB = 4

SG_CHUNK = 128
SG_GROUPS = 4
SG_GD = D // SG_GROUPS

ATT_HEADS = 16
ATT_KV = 4
ATT_DH = 64
ROPE_THETA = 10000.0
GQ_TQ = 256
GQ_CK = 256

FFN_DIM = 2816
FFN_TF = 256
FFN_NF = FFN_DIM // FFN_TF
HALO = 16

TM = 1024
SG_TM = 512
LANES = 128
NEG = -1e30
LOG2E = float(np.log2(np.e))

VMEM_LIMIT = 56 * 1024 * 1024


def _cp(sem, vmem=VMEM_LIMIT):
    return pltpu.CompilerParams(dimension_semantics=sem, vmem_limit_bytes=vmem)


def _mod_row(i, tm):
    return (i * tm) // L


def _mod_spec(k, tm, b0=0):
    return pl.BlockSpec((None, 1, D), lambda i: (_mod_row(b0 + i, tm) * N_MOD + k, 0, 0))


def _const_spec(shape, single_buffer=False):
    nd = len(shape)
    if single_buffer:
        return pl.BlockSpec(shape, lambda *_: (0,) * nd, pipeline_mode=pl.Buffered(1))
    return pl.BlockSpec(shape, lambda *_: (0,) * nd)


def _layer_spec(shape, layer, single_buffer=False):
    nd = len(shape)
    mode = dict(pipeline_mode=pl.Buffered(1)) if single_buffer else {}
    return pl.BlockSpec((None,) + tuple(shape), lambda *_: (layer,) + (0,) * nd, **mode)


def _modulate(x, g, shift, scale):
    ms = jnp.mean(x * x, axis=-1, keepdims=True)
    return x * lax.rsqrt(ms + EPS) * (g * (1.0 + scale)) + shift


def _dot(a, b):
    return jnp.dot(a, b, preferred_element_type=F32)


def _dot_nt(a, b):
    return lax.dot_general(a, b, (((1,), (1,)), ((), ())), preferred_element_type=F32)


def _ada_kernel(c_ref, w_ref, b_ref, o_ref):
    c = c_ref[...]
    s = (c * jax.nn.sigmoid(c)).astype(BF16)
    o_ref[0] = _dot(s, w_ref[0].astype(BF16)) + b_ref[0]


def _ada(cond8, w_mod, b_mod):
    tn = 2048
    n = N_MOD * D
    return pl.pallas_call(
        _ada_kernel,
        out_shape=jax.ShapeDtypeStruct((DEPTH, 8, n), F32),
        grid=(DEPTH, n // tn),
        in_specs=[
            pl.BlockSpec((8, D), lambda l, j: (0, 0)),
            pl.BlockSpec((1, D, tn), lambda l, j: (l, 0, j)),
            pl.BlockSpec((1, 1, tn), lambda l, j: (l, 0, j)),
        ],
        out_specs=pl.BlockSpec((1, 8, tn), lambda l, j: (l, 0, j)),
        compiler_params=_cp(("arbitrary", "arbitrary")),
        name="ada_params",
    )(cond8, w_mod, b_mod.reshape(DEPTH, 1, n))


def _ffn_kernel(hp_ref, h_ref, hn_ref, g_ref, sh_ref, sc_ref, gt_ref, wu_ref, wc_ref, bc_ref,
                wd_ref, gf_ref, o_ref, a_ref, gb_ref, vb_ref, *, tm, blocks_per_seq, n_compute, final_norm):
    i = pl.program_id(0)

    @pl.when(i >= n_compute)
    def _():
        o_ref[...] = h_ref[...]

    @pl.when(i < n_compute)
    def _():
        _ffn_block(hp_ref, h_ref, hn_ref, g_ref, sh_ref, sc_ref, gt_ref, wu_ref, wc_ref, bc_ref, wd_ref,
                   gf_ref, o_ref, a_ref, gb_ref, vb_ref, tm=tm, blocks_per_seq=blocks_per_seq, final_norm=final_norm)


def _ffn_block(hp_ref, h_ref, hn_ref, g_ref, sh_ref, sc_ref, gt_ref, wu_ref, wc_ref, bc_ref, wd_ref,
               gf_ref, o_ref, a_ref, gb_ref, vb_ref, *, tm, blocks_per_seq, final_norm):
    i = pl.program_id(0)
    g = g_ref[...]
    sh = sh_ref[...]
    sc = sc_ref[...]

    if blocks_per_seq == 1:
        a_ref[0:HALO, :] = jnp.zeros((HALO, D), BF16)
        a_ref[HALO + tm:, :] = jnp.zeros((HALO, D), BF16)
    else:
        seq_first = (i % blocks_per_seq) == 0
        seq_last = (i % blocks_per_seq) == blocks_per_seq - 1
        a_ref[0:HALO, :] = jnp.where(seq_first, 0.0, _modulate(hp_ref[...], g, sh, sc)).astype(BF16)
        a_ref[HALO + tm:, :] = jnp.where(seq_last, 0.0, _modulate(hn_ref[...], g, sh, sc)).astype(BF16)

    def up(j, r0=0, r1=tm):
        c0 = j * FFN_TF
        e0 = 0 if r0 == 0 else HALO + r0
        e1 = tm + 2 * HALO if r1 == tm else HALO + r1
        gb_ref[j % 2, e0:e1, :] = _dot(a_ref[e0:e1, :], wu_ref[:, c0:c0 + FFN_TF])
        vb_ref[j % 2, r0:r1, :] = _dot(a_ref[HALO + r0:HALO + r1, :],
                                       wu_ref[:, FFN_DIM + c0:FFN_DIM + c0 + FFN_TF])

    n_piece = 4 if tm == TM else 1
    pieces = [(p * tm // n_piece, (p + 1) * tm // n_piece) for p in range(n_piece)]
    for r0, r1 in pieces:
        a_ref[HALO + r0:HALO + r1, :] = _modulate(h_ref[r0:r1, :], g, sh, sc).astype(BF16)
        up(0, r0, r1)
    for j in range(FFN_NF):
        if j + 1 < FFN_NF:
            up(j + 1)
        gb = gb_ref.at[j % 2]
        c0 = j * FFN_TF
        wc = wc_ref[:, c0:c0 + FFN_TF]
        gc = (gb[HALO:HALO + tm, :] * wc[1:2]
              + gb[HALO - 1:HALO - 1 + tm, :] * wc[0:1]
              + gb[HALO + 1:HALO + 1 + tm, :] * wc[2:3]
              + bc_ref[:, c0:c0 + FFN_TF])
        u = (gc * jax.nn.sigmoid(gc) * vb_ref[j % 2]).astype(BF16)
        if j + 1 < FFN_NF:
            y = _dot(u, wd_ref[c0:c0 + FFN_TF, :])
            if j == 0:
                o_ref[...] = y
            else:
                o_ref[...] += y
        else:
            for r0, r1 in pieces:
                acc = o_ref[r0:r1, :] + _dot(u[r0:r1, :], wd_ref[c0:c0 + FFN_TF, :])
                out = h_ref[r0:r1, :] + gt_ref[...] * acc
                if final_norm:
                    ms = jnp.mean(out * out, axis=-1, keepdims=True)
                    out = out * lax.rsqrt(ms + EPS) * gf_ref[...]
                o_ref[r0:r1, :] = out


def _ffn(h, mods, g_norm, wu, wc, bc, wd, g_final, *, layer, tm, row0, n_compute, n_blocks, out_rows, in_place,
         final_norm):
    hb = tm // HALO
    b0 = row0 // tm
    last_halo = T // HALO - 1
    blocks_per_seq = L // tm if row0 == 0 else NCTX // tm
    assert not (in_place and blocks_per_seq != 1)
    kern = functools.partial(_ffn_kernel, tm=tm, blocks_per_seq=blocks_per_seq, n_compute=n_compute,
                             final_norm=final_norm)
    if blocks_per_seq == 1:
        halo = jnp.zeros((HALO, D), F32)
        halo_specs = [_const_spec((HALO, D)), _const_spec((HALO, D))]
    else:
        halo = h
        halo_specs = [pl.BlockSpec((HALO, D), lambda i: (jnp.maximum((b0 + i) * hb - 1, 0), 0)),
                      pl.BlockSpec((HALO, D), lambda i: (jnp.minimum((b0 + i + 1) * hb, last_halo), 0))]
    in_specs = [
        halo_specs[0],
        pl.BlockSpec((tm, D), lambda i: (b0 + i, 0)),
        halo_specs[1],
        _const_spec((1, D)),
        _mod_spec(3, tm, b0), _mod_spec(4, tm, b0), _mod_spec(5, tm, b0),
        _layer_spec((D, 2 * FFN_DIM), layer, single_buffer=True),
        _layer_spec((3, FFN_DIM), layer),
        _layer_spec((1, FFN_DIM), layer),
        _layer_spec((FFN_DIM, D), layer, single_buffer=True),
        _const_spec((1, D)),
    ]
    return pl.pallas_call(
        kern,
        out_shape=jax.ShapeDtypeStruct((out_rows, D), F32),
        grid=(n_blocks,),
        in_specs=in_specs,
        out_specs=pl.BlockSpec((tm, D), lambda i: (b0 + i, 0)),
        scratch_shapes=[pltpu.VMEM((tm + 2 * HALO, D), BF16), pltpu.VMEM((2, tm + 2 * HALO, FFN_TF), F32),
                        pltpu.VMEM((2, tm, FFN_TF), F32)],
        input_output_aliases={1: 0} if in_place else {},
        compiler_params=_cp(("arbitrary",)),
        name="conv_ffn_lat" if row0 == 0 else "conv_ffn_ctx",
    )(halo, h, halo, g_norm, mods, mods, mods, wu, wc, bc, wd, g_final)


def _modmm_kernel(h_ref, g_ref, sh_ref, sc_ref, w_ref, o_ref, *, tn):
    a = _modulate(h_ref[...], g_ref[...], sh_ref[...], sc_ref[...]).astype(BF16)
    for c in range(w_ref.shape[1] // tn):
        o_ref[:, c * tn:(c + 1) * tn] = _dot(a, w_ref[:, c * tn:(c + 1) * tn]).astype(BF16)


def _modmm(h, mods, g_norm, w, *, name):
    n = w.shape[1]
    return pl.pallas_call(
        functools.partial(_modmm_kernel, tn=512),
        out_shape=jax.ShapeDtypeStruct((T, n), BF16),
        grid=(T // TM,),
        in_specs=[
            pl.BlockSpec((TM, D), lambda i: (i, 0)),
            _const_spec((1, D)),
            _mod_spec(0, TM), _mod_spec(1, TM),
            _const_spec((D, n)),
        ],
        out_specs=pl.BlockSpec((TM, n), lambda i: (i, 0)),
        compiler_params=_cp(("arbitrary",)),
        name=name,
    )(h, g_norm, mods, mods, w)


def _outproj_kernel(h_ref, xl_ref, xc_ref, gt_ref, w_ref, o_ref, *, n_lat):
    i = pl.program_id(0)

    @pl.when(i < n_lat)
    def _():
        o_ref[...] = h_ref[...] + gt_ref[...] * _dot(xl_ref[...], w_ref[...])

    @pl.when(i >= n_lat)
    def _():
        o_ref[...] = h_ref[...] + gt_ref[...] * _dot(xc_ref[...], w_ref[...])


def _outproj(h, x_lat, x_ctx, mods, w, *, n_blocks, name):
    n_lat = T_LAT // TM
    return pl.pallas_call(
        functools.partial(_outproj_kernel, n_lat=n_lat),
        out_shape=jax.ShapeDtypeStruct((T, D), F32),
        grid=(n_blocks,),
        in_specs=[
            pl.BlockSpec((TM, D), lambda i: (i, 0)),
            pl.BlockSpec((TM, D), lambda i: (jnp.minimum(i, n_lat - 1), 0)),
            pl.BlockSpec((TM, D), lambda i: (jnp.maximum(i - n_lat, 0), 0)),
            _mod_spec(2, TM),
            _const_spec((D, D)),
        ],
        out_specs=pl.BlockSpec((TM, D), lambda i: (i, 0)),
        input_output_aliases={0: 0},
        compiler_params=_cp(("arbitrary",)),
        name=name,
    )(h, x_lat, x_ctx, mods, w)


def _fnet_chan_kernel(h_ref, g_ref, sh_ref, sc_ref, cs_ref, o_ref):
    a = _modulate(h_ref[...], g_ref[...], sh_ref[...], sc_ref[...]).astype(BF16)
    cs = cs_ref[...]
    for grp in range(FNET_GROUPS):
        pq = _dot(a[:, grp * FNET_GD:(grp + 1) * FNET_GD], cs)
        o_ref[0, :, grp * FNET_GD:(grp + 1) * FNET_GD] = pq[:, :FNET_GD].astype(BF16)
        o_ref[1, :, grp * FNET_GD:(grp + 1) * FNET_GD] = pq[:, FNET_GD:].astype(BF16)


def _fnet_chan(h, mods, g_norm, cs_chan):
    return pl.pallas_call(
        _fnet_chan_kernel,
        out_shape=jax.ShapeDtypeStruct((2, T, D), BF16),
        grid=(T // TM,),
        in_specs=[
            pl.BlockSpec((TM, D), lambda i: (i, 0)),
            _const_spec((1, D)),
            _mod_spec(0, TM), _mod_spec(1, TM),
            _const_spec((FNET_GD, 2 * FNET_GD)),
        ],
        out_specs=pl.BlockSpec((2, TM, D), lambda i: (0, i, 0)),
        compiler_params=_cp(("arbitrary",)),
        name="fnet_channel_dft",
    )(h, g_norm, mods, mods, cs_chan)


def _fnet_seq_kernel(hi_ref, lo_ref, pq_ref, h_ref, gt_ref, w_ref, o_ref, acc_ref, tab_ref, *, r, nkk, inv_norm):
    k = pl.program_id(2)
    use_cos = k < nkk
    c1, s1 = hi_ref[0], hi_ref[1]
    c0, s0 = lo_ref[0], lo_ref[1]
    a1 = jnp.where(use_cos, c1, -s1)
    b1 = jnp.where(use_cos, -s1, -c1)
    for fh in range(c1.shape[0]):
        tab_ref[fh * r:(fh + 1) * r, :] = (a1[fh:fh + 1, :] * c0 + b1[fh:fh + 1, :] * s0).astype(BF16)
    y = _dot(tab_ref[...], pq_ref[0])

    @pl.when(k == 0)
    def _():
        acc_ref[...] = y

    @pl.when((k > 0) & (k < 2 * nkk - 1))
    def _():
        acc_ref[...] += y

    @pl.when(k == 2 * nkk - 1)
    def _():
        f = ((acc_ref[...] + y) * inv_norm).astype(BF16)
        o_ref[...] = h_ref[...] + gt_ref[...] * _dot(f, w_ref[...])


def _fnet_seq(h, pq, angles, mods, w, *, seq, row0, mod_row_fn, name):
    hi, lo = angles
    r = lo.shape[1]
    tm = min(TM, seq)
    tk = min(2048, seq)
    nkk = seq // tk
    mt = seq // tm
    rb0 = row0 // tm
    kb0 = row0 // tk
    kern = functools.partial(_fnet_seq_kernel, r=r, nkk=nkk, inv_norm=1.0 / np.sqrt(seq * FNET_GD))
    return pl.pallas_call(
        kern,
        out_shape=jax.ShapeDtypeStruct((T, D), F32),
        grid=(B, mt, 2 * nkk),
        in_specs=[
            pl.BlockSpec((2, tm // r, tk), lambda b, m, k: (0, m, k % nkk)),
            pl.BlockSpec((2, r, tk), lambda b, m, k: (0, 0, k % nkk)),
            pl.BlockSpec((1, tk, D), lambda b, m, k: (k // nkk, kb0 + b * nkk + k % nkk, 0)),
            pl.BlockSpec((tm, D), lambda b, m, k: (rb0 + b * mt + m, 0)),
            pl.BlockSpec((None, 1, D), lambda b, m, k: (mod_row_fn(b) * N_MOD + 2, 0, 0)),
            _const_spec((D, D)),
        ],
        out_specs=pl.BlockSpec((tm, D), lambda b, m, k: (rb0 + b * mt + m, 0)),
        scratch_shapes=[pltpu.VMEM((tm, D), F32), pltpu.VMEM((tm, tk), BF16)],
        input_output_aliases={3: 0},
        compiler_params=_cp(("arbitrary", "arbitrary", "arbitrary")),
        name=name,
    )(hi, lo, pq, h, mods, w)


def _dft_angles(n):
    r = int(np.sqrt(n))
    assert r * r == n
    t = jnp.arange(n, dtype=jnp.int32)[None, :]
    f = jnp.arange(r, dtype=jnp.int32)[:, None]
    a_hi = ((f * t) % r).astype(F32) * (2.0 * np.pi / r)
    a_lo = ((f * t) % n).astype(F32) * (2.0 * np.pi / n)
    return jnp.stack([jnp.cos(a_hi), jnp.sin(a_hi)]), jnp.stack([jnp.cos(a_lo), jnp.sin(a_lo)])


def _na_bias_table(rpb):
    n_co = 2 * NA_WIN_COLS - 1
    qc = np.arange(GRID_W)[:, None]
    kc = np.arange(GRID_W)[None, :]
    cs = np.clip(qc - NA_WIN_COLS // 2, 0, GRID_W - NA_WIN_COLS)
    col_ok = (kc >= cs) & (kc < cs + NA_WIN_COLS)
    col_sel = (kc - qc + NA_WIN_COLS - 1)[None] == np.arange(n_co)[:, None, None]
    by_col = jnp.einsum("hrd,dqk->hrqk", rpb * LOG2E, jnp.asarray(col_sel, F32), precision=lax.Precision.HIGHEST)
    a = jnp.where(col_ok, by_col, NEG)
    z = jnp.zeros((NA_HEADS, 1, GRID_W, GRID_W), F32)
    return jnp.concatenate([jnp.concatenate([z, a], axis=1), jnp.concatenate([a, z], axis=1)], axis=-1)


def _na_row_tables():
    rows = L // GRID_W
    nq = L // NA_TQ
    npair = NA_KROWS // 2
    idx = np.zeros((nq, NA_QROWS * npair), np.int32)
    rmask = np.zeros((nq, NA_QROWS * npair, LANES), np.float32)
    for g in range(nq):
        k0 = int(np.clip(NA_QROWS * g - NA_WIN_ROWS // 2, 0, rows - NA_KROWS))
        for qi in range(NA_QROWS):
            r = NA_QROWS * g + qi
            rs = int(np.clip(r - NA_WIN_ROWS // 2, 0, rows - NA_WIN_ROWS))
            for m in range(npair):
                kr = k0 + 2 * m
                idx[g, qi * npair + m] = int(np.clip(kr - r + NA_WIN_ROWS, 0, 2 * NA_WIN_ROWS - 1))
                for half in range(2):
                    if not rs <= kr + half < rs + NA_WIN_ROWS:
                        rmask[g, qi * npair + m, half * NA_DH:(half + 1) * NA_DH] = NEG
    return idx, rmask


def _na_kernel(idx_ref, q_ref, k_ref, v_ref, kc_ref, vc_ref, tab_ref, rmask_ref, o_ref, snb_ref, scx_ref):
    g = pl.program_id(2)
    k0 = jnp.clip(NA_QROWS * g - NA_WIN_ROWS // 2, 0, L // GRID_W - NA_KROWS)
    start = pl.multiple_of(k0 * GRID_W, GRID_W)
    lane = lax.broadcasted_iota(jnp.int32, (1, LANES), 1)
    lower = lane < NA_DH
    npair = NA_KROWS // 2

    heads = [(t, a) for t in range(NA_HB) for a in range(2)]

    def scores(i):
        t, a = heads[i]
        cols = slice(t * LANES, (t + 1) * LANES)
        q = q_ref[:, cols]
        qm = jnp.where((lane // NA_DH) == a, q, jnp.zeros_like(q))
        bias = jnp.concatenate(
            [jnp.concatenate([tab_ref[2 * t + a, idx_ref[g, qi * npair + m]]
                              + rmask_ref[qi * npair + m:qi * npair + m + 1, :] for m in range(npair)], axis=1)
             for qi in range(NA_QROWS)], axis=0)
        snb_ref[i % 2] = _dot_nt(qm, k_ref[pl.ds(start, NA_TK), cols]) + bias
        scx_ref[i % 2] = _dot_nt(qm, kc_ref[:, cols])

    scores(0)
    prev = None
    for i, (t, a) in enumerate(heads):
        if i + 1 < len(heads):
            scores(i + 1)
        cols = slice(t * LANES, (t + 1) * LANES)
        sel = (lane // NA_DH) == a
        s_nb = snb_ref[i % 2]
        s_cx = scx_ref[i % 2]
        mx = jnp.maximum(jnp.max(s_nb, axis=-1, keepdims=True), jnp.max(s_cx, axis=-1, keepdims=True))
        vw = v_ref[pl.ds(start, NA_TK), cols]
        vc = vc_ref[:, cols]
        o = (_dot(jnp.exp2(s_nb - mx).astype(BF16), jnp.where(sel, vw, jnp.ones_like(vw)))
             + _dot(jnp.exp2(s_cx - mx).astype(BF16), jnp.where(sel, vc, jnp.ones_like(vc))))
        o = o * (1.0 / o[:, (1 - a) * NA_DH:(1 - a) * NA_DH + 1])
        if a == 0:
            prev = o
        else:
            o_ref[:, cols] = jnp.where(lower, prev, o).astype(BF16)


def _na_attend(qkv, table):
    idx, rmask = _na_row_tables()
    wb = NA_HB * LANES
    nh = D // wb
    nq = L // NA_TQ
    return pl.pallas_call(
        _na_kernel,
        out_shape=jax.ShapeDtypeStruct((T_LAT, D), BF16),
        grid=(B, nh, nq),
        in_specs=[
            pl.BlockSpec(memory_space=pltpu.SMEM),
            pl.BlockSpec((NA_TQ, wb), lambda b, h, g: (b * nq + g, h)),
            pl.BlockSpec((L, wb), lambda b, h, g: (b, nh + h)),
            pl.BlockSpec((L, wb), lambda b, h, g: (b, 2 * nh + h)),
            pl.BlockSpec((NCTX, wb), lambda b, h, g: (T_LAT // NCTX + b, nh + h)),
            pl.BlockSpec((NCTX, wb), lambda b, h, g: (T_LAT // NCTX + b, 2 * nh + h)),
            pl.BlockSpec((2 * NA_HB, 2 * NA_WIN_ROWS, GRID_W, LANES), lambda b, h, g: (h, 0, 0, 0)),
            pl.BlockSpec((None, NA_QROWS * NA_KROWS // 2, LANES), lambda b, h, g: (g, 0, 0)),
        ],
        out_specs=pl.BlockSpec((NA_TQ, wb), lambda b, h, g: (b * nq + g, h)),
        scratch_shapes=[pltpu.VMEM((2, NA_TQ, NA_TK), F32), pltpu.VMEM((2, NA_TQ, NCTX), F32)],
        compiler_params=_cp(("arbitrary", "arbitrary", "arbitrary")),
        name="na_attention",
    )(jnp.asarray(idx), qkv, qkv, qkv, qkv, qkv, table, jnp.asarray(rmask))


def _ctx_attn_kernel(q_ref, kc_ref, vc_ref, o_ref):
    lane = lax.broadcasted_iota(jnp.int32, (1, LANES), 1)
    for t in range(D // LANES):
        cols = slice(t * LANES, (t + 1) * LANES)
        kc = kc_ref[:, cols]
        vc = vc_ref[:, cols]
        q = q_ref[:, cols]
        outs = []
        for a in range(2):
            qm = jnp.where((lane // NA_DH) == a, q, jnp.zeros_like(q))
            s = _dot_nt(qm, kc)
            p = jnp.exp2(s - jnp.max(s, axis=-1, keepdims=True))
            outs.append(_dot(p.astype(BF16), vc) / jnp.sum(p, axis=-1, keepdims=True))
        o_ref[:, cols] = jnp.where(lane < NA_DH, outs[0], outs[1]).astype(BF16)


def _na_ctx_attend(qkv):
    cb = T_LAT // NCTX
    return pl.pallas_call(
        _ctx_attn_kernel,
        out_shape=jax.ShapeDtypeStruct((T_CTX, D), BF16),
        grid=(B,),
        in_specs=[
            pl.BlockSpec((NCTX, D), lambda b: (cb + b, 0)),
            pl.BlockSpec((NCTX, D), lambda b: (cb + b, 1)),
            pl.BlockSpec((NCTX, D), lambda b: (cb + b, 2)),
        ],
        out_specs=pl.BlockSpec((NCTX, D), lambda b: (b, 0)),
        compiler_params=_cp(("arbitrary",)),
        name="na_ctx_attention",
    )(qkv, qkv, qkv)


def _sg_kernel(h_ref, g_ref, sh_ref, sc_ref, gt_ref, win_ref, gv_ref, ws_ref, bs_ref, wout_ref, o_ref,
               u_ref, v_ref, t_ref):
    h = h_ref[...]
    a = _modulate(h, g_ref[...], sh_ref[...], sc_ref[...]).astype(BF16)
    tn = 512
    ssq = jnp.zeros((SG_TM, 1), F32)
    for c in range(2 * D // tn):
        z = jax.nn.gelu(_dot(a, win_ref[:, c * tn:(c + 1) * tn]), approximate=True)
        if c * tn < D:
            u_ref[:, c * tn:(c + 1) * tn] = z
        else:
            v_ref[:, c * tn - D:(c + 1) * tn - D] = z
            ssq = ssq + jnp.sum(z * z, axis=-1, keepdims=True)
    scale = lax.rsqrt(ssq * (1.0 / D) + EPS)
    vn = (v_ref[...] * scale * gv_ref[...]).astype(BF16)
    for ch in range(SG_TM // SG_CHUNK):
        r0 = ch * SG_CHUNK
        for grp in range(SG_GROUPS):
            c0 = grp * SG_GD
            mixed = _dot(ws_ref[grp], vn[r0:r0 + SG_CHUNK, c0:c0 + SG_GD])
            bs = bs_ref[grp]
            mixed = mixed + jnp.concatenate([bs, bs], axis=-1)
            t_ref[r0:r0 + SG_CHUNK, c0:c0 + SG_GD] = (u_ref[r0:r0 + SG_CHUNK, c0:c0 + SG_GD] * mixed).astype(BF16)
    o_ref[...] = h + gt_ref[...] * _dot(t_ref[...], wout_ref[...])


def _sg(h, mods, g_norm, w_in, g_v, w_s, b_s, w_out):
    return pl.pallas_call(
        _sg_kernel,
        out_shape=jax.ShapeDtypeStruct((T, D), F32),
        grid=(T // SG_TM,),
        in_specs=[
            pl.BlockSpec((SG_TM, D), lambda i: (i, 0)),
            _const_spec((1, D)),
            _mod_spec(0, SG_TM), _mod_spec(1, SG_TM), _mod_spec(2, SG_TM),
            _const_spec((D, 2 * D)),
            _const_spec((1, D)),
            _const_spec((SG_GROUPS, SG_CHUNK, SG_CHUNK)),
            _const_spec((SG_GROUPS, SG_CHUNK, LANES)),
            _const_spec((D, D)),
        ],
        out_specs=pl.BlockSpec((SG_TM, D), lambda i: (i, 0)),
        scratch_shapes=[pltpu.VMEM((SG_TM, D), F32), pltpu.VMEM((SG_TM, D), F32), pltpu.VMEM((SG_TM, D), BF16)],
        compiler_params=_cp(("arbitrary",)),
        name="spatial_gating",
    )(h, g_norm, mods, mods, mods, w_in, g_v, w_s, b_s, w_out)


def _gqa_qkv_kernel(h_ref, g_ref, sh_ref, sc_ref, w_ref, gq_ref, gk_ref, cos_ref, sin_ref,
                    q_ref, k_ref, v_ref, y_ref):
    a = _modulate(h_ref[...], g_ref[...], sh_ref[...], sc_ref[...]).astype(BF16)
    cos = cos_ref[...]
    sin = sin_ref[...]
    lane = lax.broadcasted_iota(jnp.int32, (1, LANES), 1)
    head_lanes = lane < ATT_DH
    first_half = (lane % ATT_DH) < (ATT_DH // 2)
    tn = 256
    ri = lax.broadcasted_iota(jnp.int32, (tn, tn), 0) // ATT_DH
    ci = lax.broadcasted_iota(jnp.int32, (tn, tn), 1) // ATT_DH
    head_mean = jnp.where(ri == ci, 1.0 / ATT_DH, 0.0).astype(BF16)
    n_qk = (ATT_HEADS + ATT_KV) * ATT_DH
    n_q = ATT_HEADS * ATT_DH
    n_chunk = w_ref.shape[1] // tn

    def project(c):
        y_ref[c % 2] = _dot(a, w_ref[:, c * tn:(c + 1) * tn])

    project(0)
    for c in range(n_chunk):
        if c + 1 < n_chunk:
            project(c + 1)
        y2 = y_ref[c % 2]
        if c * tn < n_qk:
            ms2 = _dot((y2 * y2).astype(BF16), head_mean)
        for s in range(tn // LANES):
            col = c * tn + s * LANES
            y = y2[:, s * LANES:(s + 1) * LANES]
            if col < n_qk:
                gain = gq_ref[...] if col < n_q else gk_ref[...]
                yn = y * lax.rsqrt(ms2[:, s * LANES:(s + 1) * LANES] + EPS) * gain
                partner = jnp.where(first_half, pltpu.roll(yn, LANES - ATT_DH // 2, 1), pltpu.roll(yn, ATT_DH // 2, 1))
                y = yn * cos + partner * sin
            if col < n_q:
                dst, base, fill = q_ref, col, 0.0
            elif col < n_qk:
                dst, base, fill = k_ref, col - n_q, 0.0
            else:
                dst, base, fill = v_ref, col - n_qk, 1.0
            dst[:, 2 * base:2 * base + LANES] = jnp.where(head_lanes, y, fill).astype(BF16)
            dst[:, 2 * base + LANES:2 * base + 2 * LANES] = jnp.where(
                head_lanes, pltpu.roll(y, ATT_DH, 1), fill).astype(BF16)


def _gqa_qkv(h, mods, g_norm, w, gq, gk, cos_t, sin_t):
    tab_spec = pl.BlockSpec((TM, LANES), lambda i: (jnp.where(i * TM >= T_LAT, L // TM, i % (L // TM)), 0))
    nq, nkv = ATT_HEADS * LANES, ATT_KV * LANES
    return pl.pallas_call(
        _gqa_qkv_kernel,
        out_shape=(jax.ShapeDtypeStruct((T, nq), BF16), jax.ShapeDtypeStruct((T, nkv), BF16),
                   jax.ShapeDtypeStruct((T, nkv), BF16)),
        grid=(T // TM,),
        in_specs=[
            pl.BlockSpec((TM, D), lambda i: (i, 0)),
            _const_spec((1, D)),
            _mod_spec(0, TM), _mod_spec(1, TM),
            _const_spec(w.shape),
            _const_spec((1, LANES)), _const_spec((1, LANES)),
            tab_spec, tab_spec,
        ],
        out_specs=(pl.BlockSpec((TM, nq), lambda i: (i, 0)), pl.BlockSpec((TM, nkv), lambda i: (i, 0)),
                   pl.BlockSpec((TM, nkv), lambda i: (i, 0))),
        scratch_shapes=[pltpu.VMEM((2, TM, 256), F32)],
        compiler_params=_cp(("arbitrary",)),
        name="gqa_qkv_rope",
    )(h, g_norm, mods, mods, w, gq, gk, cos_t, sin_t)


def _gqa_kernel(q_ref, k_ref, v_ref, kc_ref, vc_ref, o_ref):
    lane = lax.broadcasted_iota(jnp.int32, (1, LANES), 1)
    nh = ATT_HEADS // ATT_KV
    qs = jnp.concatenate([q_ref[:, i * LANES:(i + 1) * LANES] for i in range(nh)], axis=0)
    chunks = [(k_ref, v_ref, c * GQ_CK, GQ_CK) for c in range(L // GQ_CK)] + [(kc_ref, vc_ref, 0, NCTX)]
    m = jnp.full((nh * GQ_TQ, 1), -jnp.inf, F32)
    acc = jnp.zeros((nh * GQ_TQ, LANES), F32)
    for kr, vr, c0, cn in chunks:
        s = _dot_nt(qs, kr[c0:c0 + cn, :])
        m_new = jnp.maximum(m, jnp.max(s, axis=-1, keepdims=True))
        p = jnp.exp2(s - m_new).astype(BF16)
        acc = jnp.exp2(m - m_new) * acc + _dot(p, vr[c0:c0 + cn, :])
        m = m_new
    o = acc * (1.0 / acc[:, ATT_DH:ATT_DH + 1])
    for j in range(nh // 2):
        oa = o[2 * j * GQ_TQ:(2 * j + 1) * GQ_TQ]
        ob = o[(2 * j + 1) * GQ_TQ:(2 * j + 2) * GQ_TQ]
        o_ref[:, j * LANES:(j + 1) * LANES] = jnp.where(lane < ATT_DH, oa, pltpu.roll(ob, ATT_DH, 1)).astype(BF16)


def _gqa_attend(q, k, v):
    nh = ATT_HEADS // ATT_KV
    nqt = L // GQ_TQ
    cb = T_LAT // NCTX
    return pl.pallas_call(
        _gqa_kernel,
        out_shape=jax.ShapeDtypeStruct((T_LAT, D), BF16),
        grid=(B, ATT_KV, nqt),
        in_specs=[
            pl.BlockSpec((GQ_TQ, nh * LANES), lambda b, h, t: (b * nqt + t, h)),
            pl.BlockSpec((L, LANES), lambda b, h, t: (b, h)),
            pl.BlockSpec((L, LANES), lambda b, h, t: (b, h)),
            pl.BlockSpec((NCTX, LANES), lambda b, h, t: (cb + b, h)),
            pl.BlockSpec((NCTX, LANES), lambda b, h, t: (cb + b, h)),
        ],
        out_specs=pl.BlockSpec((GQ_TQ, nh * ATT_DH), lambda b, h, t: (b * nqt + t, h)),
        compiler_params=_cp(("arbitrary", "arbitrary", "arbitrary")),
        name="gqa_attention",
    )(q, k, v, k, v)


def _rope_tables():
    t = jnp.arange(L)
    row = (t // GRID_W).astype(F32)
    col = (t % GRID_W).astype(F32)
    n_freq = ATT_DH // 4
    inv_freq = ROPE_THETA ** (-jnp.arange(n_freq, dtype=F32) / n_freq)
    ang = jnp.concatenate([row[:, None] * inv_freq, col[:, None] * inv_freq], axis=-1)
    cos, sin = jnp.cos(ang), jnp.sin(ang)
    cos_t = jnp.concatenate([cos, cos, cos, cos], axis=-1)
    sin_t = jnp.concatenate([-sin, sin, -sin, sin], axis=-1)
    cos_t = jnp.concatenate([cos_t, jnp.ones((TM, LANES), F32)], axis=0)
    sin_t = jnp.concatenate([sin_t, jnp.zeros((TM, LANES), F32)], axis=0)
    return cos_t, sin_t


def kernel(x, c, ctx, c_ctx, w_mod, b_mod, g_norm_mix, g_norm_ffn, w_ffn_up, w_ffn_conv, b_ffn_conv, w_ffn_down,
           w_fnet_out, w_na_qkv, na_rel_bias, w_na_out, w_sg_in, g_sg_v, w_sg_spatial, b_sg_spatial, w_sg_out,
           w_att_qkv, g_att_q, g_att_k, w_att_out, g_final):
    assert x.shape == (B, L, D) and ctx.shape == (B, NCTX, D) and w_mod.shape[0] == DEPTH == 4

    h = jnp.concatenate([x.reshape(T_LAT, D), ctx.reshape(T_CTX, D)], axis=0)
    cond8 = jnp.concatenate([c, c_ctx[None, :], jnp.zeros((8 - B - 1, D), F32)], axis=0)
    mods_all = _ada(cond8, w_mod, b_mod)

    w_up_bf = w_ffn_up.astype(BF16)
    w_down_bf = w_ffn_down.astype(BF16)

    def layer_mods(i):
        return mods_all[i, :B + 1].reshape((B + 1) * N_MOD, 1, D)

    def ffn(h, i, *, last):
        args = (layer_mods(i), g_norm_ffn[i][None, :], w_up_bf, w_ffn_conv, b_ffn_conv[:, None, :], w_down_bf,
                g_final[None, :])
        nl = T_LAT // TM
        if last:
            return _ffn(h, *args, layer=i, tm=TM, row0=0, n_compute=nl, n_blocks=nl, out_rows=T_LAT, in_place=False,
                        final_norm=True)
        h = _ffn(h, *args, layer=i, tm=NCTX, row0=T_LAT, n_compute=B, n_blocks=B, out_rows=T, in_place=True, final_norm=False)
        return _ffn(h, *args, layer=i, tm=TM, row0=0, n_compute=nl, n_blocks=T // TM, out_rows=T, in_place=False,
                    final_norm=False)

    mods = layer_mods(0)
    ang_c = _dft_angles(FNET_GD)
    fc = jnp.arange(FNET_GD, dtype=jnp.int32)
    a_c = ((fc[:, None] * fc[None, :]) % FNET_GD).astype(F32) * (2.0 * np.pi / FNET_GD)
    cs_chan = jnp.concatenate([jnp.cos(a_c), jnp.sin(a_c)], axis=1).astype(BF16)
    pq = _fnet_chan(h, mods, g_norm_mix[0][None, :], cs_chan)
    w_f = w_fnet_out[0].astype(BF16)
    h = _fnet_seq(h, pq, _dft_angles(L), mods, w_f, seq=L, row0=0, mod_row_fn=lambda b: b, name="fnet_seq_lat")
    h = _fnet_seq(h, pq, ang_c, mods, w_f, seq=NCTX, row0=T_LAT, mod_row_fn=lambda b: B, name="fnet_seq_ctx")
    h = ffn(h, 0, last=False)

    mods = layer_mods(1)
    scale = NA_DH ** -0.5 * LOG2E
    w_qkv = jnp.concatenate([w_na_qkv[0][:, :D] * scale, w_na_qkv[0][:, D:]], axis=1).astype(BF16)
    qkv = _modmm(h, mods, g_norm_mix[1][None, :], w_qkv, name="na_qkv")
    o_lat = _na_attend(qkv, _na_bias_table(na_rel_bias[0]))
    o_ctx = _na_ctx_attend(qkv)
    h = _outproj(h, o_lat, o_ctx, mods, w_na_out[0].astype(BF16), n_blocks=T // TM, name="na_out")
    h = ffn(h, 1, last=False)

    mods = layer_mods(2)
    b_s = jnp.broadcast_to(b_sg_spatial[0][:, :, None], (SG_GROUPS, SG_CHUNK, LANES))
    h = _sg(h, mods, g_norm_mix[2][None, :], w_sg_in[0].astype(BF16), g_sg_v[0][None, :],
            w_sg_spatial[0].astype(BF16), b_s, w_sg_out[0].astype(BF16))
    h = ffn(h, 2, last=False)

    mods = layer_mods(3)
    gq = jnp.tile(g_att_q[0] * (ATT_DH ** -0.5 * LOG2E), LANES // ATT_DH)[None, :]
    gk = jnp.tile(g_att_k[0], LANES // ATT_DH)[None, :]
    cos_t, sin_t = _rope_tables()
    q, k, v = _gqa_qkv(h, mods, g_norm_mix[3][None, :], w_att_qkv[0].astype(BF16), gq, gk, cos_t, sin_t)
    o = _gqa_attend(q, k, v)
    h = _outproj(h, o, o, mods, w_att_out[0].astype(BF16), n_blocks=T_LAT // TM, name="gqa_out")
    out = ffn(h, 3, last=True)
    return out.reshape(B, L, D)
```

```python
import functools

import numpy as np
import jax
import jax.numpy as jnp
from jax import lax
from jax.experimental import pallas as pl
from jax.experimental.pallas import tpu as pltpu

F32 = jnp.float32
BF16 = jnp.bfloat16

D = 1024
B = 4
L = 4096
NCTX = 256
DEPTH = 4
GRID_W = 64
EPS = 1e-6
N_MOD = 6
T_LAT = B * L
T_CTX = B * NCTX
T = T_LAT + T_CTX

FNET_GROUPS = 4
FNET_GD = D // FNET_GROUPS

NA_HEADS = 16
NA_DH = D // NA_HEADS
NA_WIN_ROWS = 8
NA_WIN_COLS = 16
NA_QROWS = 4
NA_KROWS = 12
NA_TQ = NA_QROWS * GRID_W
NA_TK = NA_KROWS * GRID_W
NA_HB = 8

SG_CHUNK = 128
SG_GROUPS = 4
SG_GD = D // SG_GROUPS

ATT_HEADS = 16
ATT_KV = 4
ATT_DH = 64
ROPE_THETA = 10000.0
GQ_TQ = 256
GQ_CK = 256

FFN_DIM = 2816
FFN_TF = 256
FFN_NF = FFN_DIM // FFN_TF
HALO = 16

TM = 1024
SG_TM = 512
LANES = 128
NEG = -1e30
LOG2E = float(np.log2(np.e))

VMEM_LIMIT = 56 * 1024 * 1024


def _cp(sem, vmem=VMEM_LIMIT):
    return pltpu.CompilerParams(dimension_semantics=sem, vmem_limit_bytes=vmem)


def _mod_row(i, tm):
    return (i * tm) // L


def _mod_spec(k, tm, b0=0):
    return pl.BlockSpec((None, 1, D), lambda i: (_mod_row(b0 + i, tm) * N_MOD + k, 0, 0))


def _const_spec(shape, single_buffer=False):
    nd = len(shape)
    if single_buffer:
        return pl.BlockSpec(shape, lambda *_: (0,) * nd, pipeline_mode=pl.Buffered(1))
    return pl.BlockSpec(shape, lambda *_: (0,) * nd)


def _layer_spec(shape, layer, single_buffer=False):
    nd = len(shape)
    mode = dict(pipeline_mode=pl.Buffered(1)) if single_buffer else {}
    return pl.BlockSpec((None,) + tuple(shape), lambda *_: (layer,) + (0,) * nd, **mode)


def _modulate(x, g, shift, scale):
    ms = jnp.mean(x * x, axis=-1, keepdims=True)
    return x * lax.rsqrt(ms + EPS) * (g * (1.0 + scale)) + shift


def _dot(a, b):
    return jnp.dot(a, b, preferred_element_type=F32)


def _dot_nt(a, b):
    return lax.dot_general(a, b, (((1,), (1,)), ((), ())), preferred_element_type=F32)


def _ada_kernel(c_ref, w_ref, b_ref, o_ref):
    c = c_ref[...]
    s = (c * jax.nn.sigmoid(c)).astype(BF16)
    o_ref[0] = _dot(s, w_ref[0].astype(BF16)) + b_ref[0]


def _ada(cond8, w_mod, b_mod):
    tn = 2048
    n = N_MOD * D
    return pl.pallas_call(
        _ada_kernel,
        out_shape=jax.ShapeDtypeStruct((DEPTH, 8, n), F32),
        grid=(DEPTH, n // tn),
        in_specs=[
            pl.BlockSpec((8, D), lambda l, j: (0, 0)),
            pl.BlockSpec((1, D, tn), lambda l, j: (l, 0, j)),
            pl.BlockSpec((1, 1, tn), lambda l, j: (l, 0, j)),
        ],
        out_specs=pl.BlockSpec((1, 8, tn), lambda l, j: (l, 0, j)),
        compiler_params=_cp(("arbitrary", "arbitrary")),
        name="ada_params",
    )(cond8, w_mod, b_mod.reshape(DEPTH, 1, n))


def _ffn_kernel(hp_ref, h_ref, hn_ref, g_ref, sh_ref, sc_ref, gt_ref, wu_ref, wc_ref, bc_ref,
                wd_ref, gf_ref, o_ref, a_ref, gb_ref, vb_ref, *, tm, blocks_per_seq, n_compute, final_norm):
    i = pl.program_id(0)

    @pl.when(i >= n_compute)
    def _():
        o_ref[...] = h_ref[...]

    @pl.when(i < n_compute)
    def _():
        _ffn_block(hp_ref, h_ref, hn_ref, g_ref, sh_ref, sc_ref, gt_ref, wu_ref, wc_ref, bc_ref, wd_ref,
                   gf_ref, o_ref, a_ref, gb_ref, vb_ref, tm=tm, blocks_per_seq=blocks_per_seq, final_norm=final_norm)


def _ffn_block(hp_ref, h_ref, hn_ref, g_ref, sh_ref, sc_ref, gt_ref, wu_ref, wc_ref, bc_ref, wd_ref,
               gf_ref, o_ref, a_ref, gb_ref, vb_ref, *, tm, blocks_per_seq, final_norm):
    i = pl.program_id(0)
    g = g_ref[...]
    sh = sh_ref[...]
    sc = sc_ref[...]

    if blocks_per_seq == 1:
        a_ref[0:HALO, :] = jnp.zeros((HALO, D), BF16)
        a_ref[HALO + tm:, :] = jnp.zeros((HALO, D), BF16)
    else:
        seq_first = (i % blocks_per_seq) == 0
        seq_last = (i % blocks_per_seq) == blocks_per_seq - 1
        a_ref[0:HALO, :] = jnp.where(seq_first, 0.0, _modulate(hp_ref[...], g, sh, sc)).astype(BF16)
        a_ref[HALO + tm:, :] = jnp.where(seq_last, 0.0, _modulate(hn_ref[...], g, sh, sc)).astype(BF16)

    def up(j, r0=0, r1=tm):
        c0 = j * FFN_TF
        e0 = 0 if r0 == 0 else HALO + r0
        e1 = tm + 2 * HALO if r1 == tm else HALO + r1
        gb_ref[j % 2, e0:e1, :] = _dot(a_ref[e0:e1, :], wu_ref[:, c0:c0 + FFN_TF])
        vb_ref[j % 2, r0:r1, :] = _dot(a_ref[HALO + r0:HALO + r1, :],
                                       wu_ref[:, FFN_DIM + c0:FFN_DIM + c0 + FFN_TF])

    n_piece = 4 if tm == TM else 1
    pieces = [(p * tm // n_piece, (p + 1) * tm // n_piece) for p in range(n_piece)]
    for r0, r1 in pieces:
        a_ref[HALO + r0:HALO + r1, :] = _modulate(h_ref[r0:r1, :], g, sh, sc).astype(BF16)
        up(0, r0, r1)
    for j in range(FFN_NF):
        if j + 1 < FFN_NF:
            up(j + 1)
        gb = gb_ref.at[j % 2]
        c0 = j * FFN_TF
        wc = wc_ref[:, c0:c0 + FFN_TF]
        gc = (gb[HALO:HALO + tm, :] * wc[1:2]
              + gb[HALO - 1:HALO - 1 + tm, :] * wc[0:1]
              + gb[HALO + 1:HALO + 1 + tm, :] * wc[2:3]
              + bc_ref[:, c0:c0 + FFN_TF])
        u = (gc * jax.nn.sigmoid(gc) * vb_ref[j % 2]).astype(BF16)
        if j + 1 < FFN_NF:
            y = _dot(u, wd_ref[c0:c0 + FFN_TF, :])
            if j == 0:
                o_ref[...] = y
            else:
                o_ref[...] += y
        else:
            for r0, r1 in pieces:
                acc = o_ref[r0:r1, :] + _dot(u[r0:r1, :], wd_ref[c0:c0 + FFN_TF, :])
                out = h_ref[r0:r1, :] + gt_ref[...] * acc
                if final_norm:
                    ms = jnp.mean(out * out, axis=-1, keepdims=True)
                    out = out * lax.rsqrt(ms + EPS) * gf_ref[...]
                o_ref[r0:r1, :] = out


def _ffn(h, mods, g_norm, wu, wc, bc, wd, g_final, *, layer, tm, row0, n_compute, n_blocks, out_rows, in_place,
         final_norm):
    hb = tm // HALO
    b0 = row0 // tm
    last_halo = T // HALO - 1
    blocks_per_seq = L // tm if row0 == 0 else NCTX // tm
    assert not (in_place and blocks_per_seq != 1)
    kern = functools.partial(_ffn_kernel, tm=tm, blocks_per_seq=blocks_per_seq, n_compute=n_compute,
                             final_norm=final_norm)
    if blocks_per_seq == 1:
        halo = jnp.zeros((HALO, D), F32)
        halo_specs = [_const_spec((HALO, D)), _const_spec((HALO, D))]
    else:
        halo = h
        halo_specs = [pl.BlockSpec((HALO, D), lambda i: (jnp.maximum((b0 + i) * hb - 1, 0), 0)),
                      pl.BlockSpec((HALO, D), lambda i: (jnp.minimum((b0 + i + 1) * hb, last_halo), 0))]
    in_specs = [
        halo_specs[0],
        pl.BlockSpec((tm, D), lambda i: (b0 + i, 0)),
        halo_specs[1],
        _const_spec((1, D)),
        _mod_spec(3, tm, b0), _mod_spec(4, tm, b0), _mod_spec(5, tm, b0),
        _layer_spec((D, 2 * FFN_DIM), layer, single_buffer=True),
        _layer_spec((3, FFN_DIM), layer),
        _layer_spec((1, FFN_DIM), layer),
        _layer_spec((FFN_DIM, D), layer, single_buffer=True),
        _const_spec((1, D)),
    ]
    return pl.pallas_call(
        kern,
        out_shape=jax.ShapeDtypeStruct((out_rows, D), F32),
        grid=(n_blocks,),
        in_specs=in_specs,
        out_specs=pl.BlockSpec((tm, D), lambda i: (b0 + i, 0)),
        scratch_shapes=[pltpu.VMEM((tm + 2 * HALO, D), BF16), pltpu.VMEM((2, tm + 2 * HALO, FFN_TF), F32),
                        pltpu.VMEM((2, tm, FFN_TF), F32)],
        input_output_aliases={1: 0} if in_place else {},
        compiler_params=_cp(("arbitrary",)),
        name="conv_ffn_lat" if row0 == 0 else "conv_ffn_ctx",
    )(halo, h, halo, g_norm, mods, mods, mods, wu, wc, bc, wd, g_final)


def _modmm_kernel(h_ref, g_ref, sh_ref, sc_ref, w_ref, o_ref, *, tn):
    a = _modulate(h_ref[...], g_ref[...], sh_ref[...], sc_ref[...]).astype(BF16)
    for c in range(w_ref.shape[1] // tn):
        o_ref[:, c * tn:(c + 1) * tn] = _dot(a, w_ref[:, c * tn:(c + 1) * tn]).astype(BF16)


def _modmm(h, mods, g_norm, w, *, name):
    n = w.shape[1]
    return pl.pallas_call(
        functools.partial(_modmm_kernel, tn=512),
        out_shape=jax.ShapeDtypeStruct((T, n), BF16),
        grid=(T // TM,),
        in_specs=[
            pl.BlockSpec((TM, D), lambda i: (i, 0)),
            _const_spec((1, D)),
            _mod_spec(0, TM), _mod_spec(1, TM),
            _const_spec((D, n)),
        ],
        out_specs=pl.BlockSpec((TM, n), lambda i: (i, 0)),
        compiler_params=_cp(("arbitrary",)),
        name=name,
    )(h, g_norm, mods, mods, w)


def _outproj_kernel(h_ref, xl_ref, xc_ref, gt_ref, w_ref, o_ref, *, n_lat):
    i = pl.program_id(0)

    @pl.when(i < n_lat)
    def _():
        o_ref[...] = h_ref[...] + gt_ref[...] * _dot(xl_ref[...], w_ref[...])

    @pl.when(i >= n_lat)
    def _():
        o_ref[...] = h_ref[...] + gt_ref[...] * _dot(xc_ref[...], w_ref[...])


def _outproj(h, x_lat, x_ctx, mods, w, *, n_blocks, name):
    n_lat = T_LAT // TM
    return pl.pallas_call(
        functools.partial(_outproj_kernel, n_lat=n_lat),
        out_shape=jax.ShapeDtypeStruct((T, D), F32),
        grid=(n_blocks,),
        in_specs=[
            pl.BlockSpec((TM, D), lambda i: (i, 0)),
            pl.BlockSpec((TM, D), lambda i: (jnp.minimum(i, n_lat - 1), 0)),
            pl.BlockSpec((TM, D), lambda i: (jnp.maximum(i - n_lat, 0), 0)),
            _mod_spec(2, TM),
            _const_spec((D, D)),
        ],
        out_specs=pl.BlockSpec((TM, D), lambda i: (i, 0)),
        input_output_aliases={0: 0},
        compiler_params=_cp(("arbitrary",)),
        name=name,
    )(h, x_lat, x_ctx, mods, w)


def _fnet_chan_kernel(h_ref, g_ref, sh_ref, sc_ref, cs_ref, o_ref):
    a = _modulate(h_ref[...], g_ref[...], sh_ref[...], sc_ref[...]).astype(BF16)
    cs = cs_ref[...]
    for grp in range(FNET_GROUPS):
        pq = _dot(a[:, grp * FNET_GD:(grp + 1) * FNET_GD], cs)
        o_ref[0, :, grp * FNET_GD:(grp + 1) * FNET_GD] = pq[:, :FNET_GD].astype(BF16)
        o_ref[1, :, grp * FNET_GD:(grp + 1) * FNET_GD] = pq[:, FNET_GD:].astype(BF16)


def _fnet_chan(h, mods, g_norm, cs_chan):
    return pl.pallas_call(
        _fnet_chan_kernel,
        out_shape=jax.ShapeDtypeStruct((2, T, D), BF16),
        grid=(T // TM,),
        in_specs=[
            pl.BlockSpec((TM, D), lambda i: (i, 0)),
            _const_spec((1, D)),
            _mod_spec(0, TM), _mod_spec(1, TM),
            _const_spec((FNET_GD, 2 * FNET_GD)),
        ],
        out_specs=pl.BlockSpec((2, TM, D), lambda i: (0, i, 0)),
        compiler_params=_cp(("arbitrary",)),
        name="fnet_channel_dft",
    )(h, g_norm, mods, mods, cs_chan)


def _fnet_seq_kernel(hi_ref, lo_ref, pq_ref, h_ref, gt_ref, w_ref, o_ref, acc_ref, tab_ref, *, r, nkk, inv_norm):
    k = pl.program_id(2)
    use_cos = k < nkk
    c1, s1 = hi_ref[0], hi_ref[1]
    c0, s0 = lo_ref[0], lo_ref[1]
    a1 = jnp.where(use_cos, c1, -s1)
    b1 = jnp.where(use_cos, -s1, -c1)
    for fh in range(c1.shape[0]):
        tab_ref[fh * r:(fh + 1) * r, :] = (a1[fh:fh + 1, :] * c0 + b1[fh:fh + 1, :] * s0).astype(BF16)
    y = _dot(tab_ref[...], pq_ref[0])

    @pl.when(k == 0)
    def _():
        acc_ref[...] = y

    @pl.when((k > 0) & (k < 2 * nkk - 1))
    def _():
        acc_ref[...] += y

    @pl.when(k == 2 * nkk - 1)
    def _():
        f = ((acc_ref[...] + y) * inv_norm).astype(BF16)
        o_ref[...] = h_ref[...] + gt_ref[...] * _dot(f, w_ref[...])


def _fnet_seq(h, pq, angles, mods, w, *, seq, row0, mod_row_fn, name):
    hi, lo = angles
    r = lo.shape[1]
    tm = min(TM, seq)
    tk = min(2048, seq)
    nkk = seq // tk
    mt = seq // tm
    rb0 = row0 // tm
    kb0 = row0 // tk
    kern = functools.partial(_fnet_seq_kernel, r=r, nkk=nkk, inv_norm=1.0 / np.sqrt(seq * FNET_GD))
    return pl.pallas_call(
        kern,
        out_shape=jax.ShapeDtypeStruct((T, D), F32),
        grid=(B, mt, 2 * nkk),
        in_specs=[
            pl.BlockSpec((2, tm // r, tk), lambda b, m, k: (0, m, k % nkk)),
            pl.BlockSpec((2, r, tk), lambda b, m, k: (0, 0, k % nkk)),
            pl.BlockSpec((1, tk, D), lambda b, m, k: (k // nkk, kb0 + b * nkk + k % nkk, 0)),
            pl.BlockSpec((tm, D), lambda b, m, k: (rb0 + b * mt + m, 0)),
            pl.BlockSpec((None, 1, D), lambda b, m, k: (mod_row_fn(b) * N_MOD + 2, 0, 0)),
            _const_spec((D, D)),
        ],
        out_specs=pl.BlockSpec((tm, D), lambda b, m, k: (rb0 + b * mt + m, 0)),
        scratch_shapes=[pltpu.VMEM((tm, D), F32), pltpu.VMEM((tm, tk), BF16)],
        input_output_aliases={3: 0},
        compiler_params=_cp(("arbitrary", "arbitrary", "arbitrary")),
        name=name,
    )(hi, lo, pq, h, mods, w)


def _dft_angles(n):
    r = int(np.sqrt(n))
    assert r * r == n
    t = jnp.arange(n, dtype=jnp.int32)[None, :]
    f = jnp.arange(r, dtype=jnp.int32)[:, None]
    a_hi = ((f * t) % r).astype(F32) * (2.0 * np.pi / r)
    a_lo = ((f * t) % n).astype(F32) * (2.0 * np.pi / n)
    return jnp.stack([jnp.cos(a_hi), jnp.sin(a_hi)]), jnp.stack([jnp.cos(a_lo), jnp.sin(a_lo)])


def _na_bias_table(rpb):
    n_co = 2 * NA_WIN_COLS - 1
    qc = np.arange(GRID_W)[:, None]
    kc = np.arange(GRID_W)[None, :]
    cs = np.clip(qc - NA_WIN_COLS // 2, 0, GRID_W - NA_WIN_COLS)
    col_ok = (kc >= cs) & (kc < cs + NA_WIN_COLS)
    col_sel = (kc - qc + NA_WIN_COLS - 1)[None] == np.arange(n_co)[:, None, None]
    by_col = jnp.einsum("hrd,dqk->hrqk", rpb * LOG2E, jnp.asarray(col_sel, F32), precision=lax.Precision.HIGHEST)
    a = jnp.where(col_ok, by_col, NEG)
    z = jnp.zeros((NA_HEADS, 1, GRID_W, GRID_W), F32)
    return jnp.concatenate([jnp.concatenate([z, a], axis=1), jnp.concatenate([a, z], axis=1)], axis=-1)


def _na_row_tables():
    rows = L // GRID_W
    nq = L // NA_TQ
    npair = NA_KROWS // 2
    idx = np.zeros((nq, NA_QROWS * npair), np.int32)
    rmask = np.zeros((nq, NA_QROWS * npair, LANES), np.float32)
    for g in range(nq):
        k0 = int(np.clip(NA_QROWS * g - NA_WIN_ROWS // 2, 0, rows - NA_KROWS))
        for qi in range(NA_QROWS):
            r = NA_QROWS * g + qi
            rs = int(np.clip(r - NA_WIN_ROWS // 2, 0, rows - NA_WIN_ROWS))
            for m in range(npair):
                kr = k0 + 2 * m
                idx[g, qi * npair + m] = int(np.clip(kr - r + NA_WIN_ROWS, 0, 2 * NA_WIN_ROWS - 1))
                for half in range(2):
                    if not rs <= kr + half < rs + NA_WIN_ROWS:
                        rmask[g, qi * npair + m, half * NA_DH:(half + 1) * NA_DH] = NEG
    return idx, rmask


def _na_kernel(idx_ref, q_ref, k_ref, v_ref, kc_ref, vc_ref, tab_ref, rmask_ref, o_ref, snb_ref, scx_ref):
    g = pl.program_id(2)
    k0 = jnp.clip(NA_QROWS * g - NA_WIN_ROWS // 2, 0, L // GRID_W - NA_KROWS)
    start = pl.multiple_of(k0 * GRID_W, GRID_W)
    lane = lax.broadcasted_iota(jnp.int32, (1, LANES), 1)
    lower = lane < NA_DH
    npair = NA_KROWS // 2

    heads = [(t, a) for t in range(NA_HB) for a in range(2)]

    def scores(i):
        t, a = heads[i]
        cols = slice(t * LANES, (t + 1) * LANES)
        q = q_ref[:, cols]
        qm = jnp.where((lane // NA_DH) == a, q, jnp.zeros_like(q))
        bias = jnp.concatenate(
            [jnp.concatenate([tab_ref[2 * t + a, idx_ref[g, qi * npair + m]]
                              + rmask_ref[qi * npair + m:qi * npair + m + 1, :] for m in range(npair)], axis=1)
             for qi in range(NA_QROWS)], axis=0)
        snb_ref[i % 2] = _dot_nt(qm, k_ref[pl.ds(start, NA_TK), cols]) + bias
        scx_ref[i % 2] = _dot_nt(qm, kc_ref[:, cols])

    scores(0)
    prev = None
    for i, (t, a) in enumerate(heads):
        if i + 1 < len(heads):
            scores(i + 1)
        cols = slice(t * LANES, (t + 1) * LANES)
        sel = (lane // NA_DH) == a
        s_nb = snb_ref[i % 2]
        s_cx = scx_ref[i % 2]
        mx = jnp.maximum(jnp.max(s_nb, axis=-1, keepdims=True), jnp.max(s_cx, axis=-1, keepdims=True))
        vw = v_ref[pl.ds(start, NA_TK), cols]
        vc = vc_ref[:, cols]
        o = (_dot(jnp.exp2(s_nb - mx).astype(BF16), jnp.where(sel, vw, jnp.ones_like(vw)))
             + _dot(jnp.exp2(s_cx - mx).astype(BF16), jnp.where(sel, vc, jnp.ones_like(vc))))
        o = o * (1.0 / o[:, (1 - a) * NA_DH:(1 - a) * NA_DH + 1])
        if a == 0:
            prev = o
        else:
            o_ref[:, cols] = jnp.where(lower, prev, o).astype(BF16)


def _na_attend(qkv, table):
    idx, rmask = _na_row_tables()
    wb = NA_HB * LANES
    nh = D // wb
    nq = L // NA_TQ
    return pl.pallas_call(
        _na_kernel,
        out_shape=jax.ShapeDtypeStruct((T_LAT, D), BF16),
        grid=(B, nh, nq),
        in_specs=[
            pl.BlockSpec(memory_space=pltpu.SMEM),
            pl.BlockSpec((NA_TQ, wb), lambda b, h, g: (b * nq + g, h)),
            pl.BlockSpec((L, wb), lambda b, h, g: (b, nh + h)),
            pl.BlockSpec((L, wb), lambda b, h, g: (b, 2 * nh + h)),
            pl.BlockSpec((NCTX, wb), lambda b, h, g: (T_LAT // NCTX + b, nh + h)),
            pl.BlockSpec((NCTX, wb), lambda b, h, g: (T_LAT // NCTX + b, 2 * nh + h)),
            pl.BlockSpec((2 * NA_HB, 2 * NA_WIN_ROWS, GRID_W, LANES), lambda b, h, g: (h, 0, 0, 0),
                         pipeline_mode=pl.Buffered(1)),
            pl.BlockSpec((None, NA_QROWS * NA_KROWS // 2, LANES), lambda b, h, g: (g, 0, 0)),
        ],
        out_specs=pl.BlockSpec((NA_TQ, wb), lambda b, h, g: (b * nq + g, h)),
        scratch_shapes=[pltpu.VMEM((2, NA_TQ, NA_TK), F32), pltpu.VMEM((2, NA_TQ, NCTX), F32)],
        compiler_params=_cp(("arbitrary", "arbitrary", "arbitrary")),
        name="na_attention",
    )(jnp.asarray(idx), qkv, qkv, qkv, qkv, qkv, table, jnp.asarray(rmask))


def _ctx_attn_kernel(q_ref, kc_ref, vc_ref, o_ref):
    lane = lax.broadcasted_iota(jnp.int32, (1, LANES), 1)
    for t in range(D // LANES):
        cols = slice(t * LANES, (t + 1) * LANES)
        kc = kc_ref[:, cols]
        vc = vc_ref[:, cols]
        q = q_ref[:, cols]
        outs = []
        for a in range(2):
            qm = jnp.where((lane // NA_DH) == a, q, jnp.zeros_like(q))
            s = _dot_nt(qm, kc)
            p = jnp.exp2(s - jnp.max(s, axis=-1, keepdims=True))
            outs.append(_dot(p.astype(BF16), vc) / jnp.sum(p, axis=-1, keepdims=True))
        o_ref[:, cols] = jnp.where(lane < NA_DH, outs[0], outs[1]).astype(BF16)


def _na_ctx_attend(qkv):
    cb = T_LAT // NCTX
    return pl.pallas_call(
        _ctx_attn_kernel,
        out_shape=jax.ShapeDtypeStruct((T_CTX, D), BF16),
        grid=(B,),
        in_specs=[
            pl.BlockSpec((NCTX, D), lambda b: (cb + b, 0)),
            pl.BlockSpec((NCTX, D), lambda b: (cb + b, 1)),
            pl.BlockSpec((NCTX, D), lambda b: (cb + b, 2)),
        ],
        out_specs=pl.BlockSpec((NCTX, D), lambda b: (b, 0)),
        compiler_params=_cp(("arbitrary",)),
        name="na_ctx_attention",
    )(qkv, qkv, qkv)


def _sg_kernel(h_ref, g_ref, sh_ref, sc_ref, gt_ref, win_ref, gv_ref, ws_ref, bs_ref, wout_ref, o_ref,
               u_ref, v_ref, t_ref):
    h = h_ref[...]
    a = _modulate(h, g_ref[...], sh_ref[...], sc_ref[...]).astype(BF16)
    tn = 512
    ssq = jnp.zeros((SG_TM, 1), F32)
    for c in range(2 * D // tn):
        z = jax.nn.gelu(_dot(a, win_ref[:, c * tn:(c + 1) * tn]), approximate=True)
        if c * tn < D:
            u_ref[:, c * tn:(c + 1) * tn] = z
        else:
            v_ref[:, c * tn - D:(c + 1) * tn - D] = z
            ssq = ssq + jnp.sum(z * z, axis=-1, keepdims=True)
    scale = lax.rsqrt(ssq * (1.0 / D) + EPS)
    vn = (v_ref[...] * scale * gv_ref[...]).astype(BF16)
    for ch in range(SG_TM // SG_CHUNK):
        r0 = ch * SG_CHUNK
        for grp in range(SG_GROUPS):
            c0 = grp * SG_GD
            mixed = _dot(ws_ref[grp], vn[r0:r0 + SG_CHUNK, c0:c0 + SG_GD])
            bs = bs_ref[grp]
            mixed = mixed + jnp.concatenate([bs, bs], axis=-1)
            t_ref[r0:r0 + SG_CHUNK, c0:c0 + SG_GD] = (u_ref[r0:r0 + SG_CHUNK, c0:c0 + SG_GD] * mixed).astype(BF16)
    o_ref[...] = h + gt_ref[...] * _dot(t_ref[...], wout_ref[...])


def _sg(h, mods, g_norm, w_in, g_v, w_s, b_s, w_out):
    return pl.pallas_call(
        _sg_kernel,
        out_shape=jax.ShapeDtypeStruct((T, D), F32),
        grid=(T // SG_TM,),
        in_specs=[
            pl.BlockSpec((SG_TM, D), lambda i: (i, 0)),
            _const_spec((1, D)),
            _mod_spec(0, SG_TM), _mod_spec(1, SG_TM), _mod_spec(2, SG_TM),
            _const_spec((D, 2 * D)),
            _const_spec((1, D)),
            _const_spec((SG_GROUPS, SG_CHUNK, SG_CHUNK)),
            _const_spec((SG_GROUPS, SG_CHUNK, LANES)),
            _const_spec((D, D)),
        ],
        out_specs=pl.BlockSpec((SG_TM, D), lambda i: (i, 0)),
        scratch_shapes=[pltpu.VMEM((SG_TM, D), F32), pltpu.VMEM((SG_TM, D), F32), pltpu.VMEM((SG_TM, D), BF16)],
        compiler_params=_cp(("arbitrary",)),
        name="spatial_gating",
    )(h, g_norm, mods, mods, mods, w_in, g_v, w_s, b_s, w_out)


def _gqa_qkv_kernel(h_ref, g_ref, sh_ref, sc_ref, w_ref, gq_ref, gk_ref, cos_ref, sin_ref,
                    q_ref, k_ref, v_ref, y_ref):
    a = _modulate(h_ref[...], g_ref[...], sh_ref[...], sc_ref[...]).astype(BF16)
    cos = cos_ref[...]
    sin = sin_ref[...]
    lane = lax.broadcasted_iota(jnp.int32, (1, LANES), 1)
    head_lanes = lane < ATT_DH
    first_half = (lane % ATT_DH) < (ATT_DH // 2)
    tn = 256
    ri = lax.broadcasted_iota(jnp.int32, (tn, tn), 0) // ATT_DH
    ci = lax.broadcasted_iota(jnp.int32, (tn, tn), 1) // ATT_DH
    head_mean = jnp.where(ri == ci, 1.0 / ATT_DH, 0.0).astype(BF16)
    n_qk = (ATT_HEADS + ATT_KV) * ATT_DH
    n_q = ATT_HEADS * ATT_DH
    n_chunk = w_ref.shape[1] // tn

    def project(c):
        y_ref[c % 2] = _dot(a, w_ref[:, c * tn:(c + 1) * tn])

    project(0)
    for c in range(n_chunk):
        if c + 1 < n_chunk:
            project(c + 1)
        y2 = y_ref[c % 2]
        if c * tn < n_qk:
            ms2 = _dot((y2 * y2).astype(BF16), head_mean)
        for s in range(tn // LANES):
            col = c * tn + s * LANES
            y = y2[:, s * LANES:(s + 1) * LANES]
            if col < n_qk:
                gain = gq_ref[...] if col < n_q else gk_ref[...]
                yn = y * lax.rsqrt(ms2[:, s * LANES:(s + 1) * LANES] + EPS) * gain
                partner = jnp.where(first_half, pltpu.roll(yn, LANES - ATT_DH // 2, 1), pltpu.roll(yn, ATT_DH // 2, 1))
                y = yn * cos + partner * sin
            if col < n_q:
                dst, base, fill = q_ref, col, 0.0
            elif col < n_qk:
                dst, base, fill = k_ref, col - n_q, 0.0
            else:
                dst, base, fill = v_ref, col - n_qk, 1.0
            dst[:, 2 * base:2 * base + LANES] = jnp.where(head_lanes, y, fill).astype(BF16)
            dst[:, 2 * base + LANES:2 * base + 2 * LANES] = jnp.where(
                head_lanes, pltpu.roll(y, ATT_DH, 1), fill).astype(BF16)


def _gqa_qkv(h, mods, g_norm, w, gq, gk, cos_t, sin_t):
    tab_spec = pl.BlockSpec((TM, LANES), lambda i: (jnp.where(i * TM >= T_LAT, L // TM, i % (L // TM)), 0))
    nq, nkv = ATT_HEADS * LANES, ATT_KV * LANES
    return pl.pallas_call(
        _gqa_qkv_kernel,
        out_shape=(jax.ShapeDtypeStruct((T, nq), BF16), jax.ShapeDtypeStruct((T, nkv), BF16),
                   jax.ShapeDtypeStruct((T, nkv), BF16)),
        grid=(T // TM,),
        in_specs=[
            pl.BlockSpec((TM, D), lambda i: (i, 0)),
            _const_spec((1, D)),
            _mod_spec(0, TM), _mod_spec(1, TM),
            _const_spec(w.shape),
            _const_spec((1, LANES)), _const_spec((1, LANES)),
            tab_spec, tab_spec,
        ],
        out_specs=(pl.BlockSpec((TM, nq), lambda i: (i, 0)), pl.BlockSpec((TM, nkv), lambda i: (i, 0)),
                   pl.BlockSpec((TM, nkv), lambda i: (i, 0))),
        scratch_shapes=[pltpu.VMEM((2, TM, 256), F32)],
        compiler_params=_cp(("arbitrary",)),
        name="gqa_qkv_rope",
    )(h, g_norm, mods, mods, w, gq, gk, cos_t, sin_t)


def _gqa_kernel(q_ref, k_ref, v_ref, kc_ref, vc_ref, o_ref):
    lane = lax.broadcasted_iota(jnp.int32, (1, LANES), 1)
    nh = ATT_HEADS // ATT_KV
    qs = jnp.concatenate([q_ref[:, i * LANES:(i + 1) * LANES] for i in range(nh)], axis=0)
    chunks = [(k_ref, v_ref, c * GQ_CK, GQ_CK) for c in range(L // GQ_CK)] + [(kc_ref, vc_ref, 0, NCTX)]
    m = jnp.full((nh * GQ_TQ, 1), -jnp.inf, F32)
    acc = jnp.zeros((nh * GQ_TQ, LANES), F32)
    for kr, vr, c0, cn in chunks:
        s = _dot_nt(qs, kr[c0:c0 + cn, :])
        m_new = jnp.maximum(m, jnp.max(s, axis=-1, keepdims=True))
        p = jnp.exp2(s - m_new).astype(BF16)
        acc = jnp.exp2(m - m_new) * acc + _dot(p, vr[c0:c0 + cn, :])
        m = m_new
    o = acc * (1.0 / acc[:, ATT_DH:ATT_DH + 1])
    for j in range(nh // 2):
        oa = o[2 * j * GQ_TQ:(2 * j + 1) * GQ_TQ]
        ob = o[(2 * j + 1) * GQ_TQ:(2 * j + 2) * GQ_TQ]
        o_ref[:, j * LANES:(j + 1) * LANES] = jnp.where(lane < ATT_DH, oa, pltpu.roll(ob, ATT_DH, 1)).astype(BF16)


def _gqa_attend(q, k, v):
    nh = ATT_HEADS // ATT_KV
    nqt = L // GQ_TQ
    cb = T_LAT // NCTX
    return pl.pallas_call(
        _gqa_kernel,
        out_shape=jax.ShapeDtypeStruct((T_LAT, D), BF16),
        grid=(B, ATT_KV, nqt),
        in_specs=[
            pl.BlockSpec((GQ_TQ, nh * LANES), lambda b, h, t: (b * nqt + t, h)),
            pl.BlockSpec((L, LANES), lambda b, h, t: (b, h)),
            pl.BlockSpec((L, LANES), lambda b, h, t: (b, h)),
            pl.BlockSpec((NCTX, LANES), lambda b, h, t: (cb + b, h)),
            pl.BlockSpec((NCTX, LANES), lambda b, h, t: (cb + b, h)),
        ],
        out_specs=pl.BlockSpec((GQ_TQ, nh * ATT_DH), lambda b, h, t: (b * nqt + t, h)),
        compiler_params=_cp(("arbitrary", "arbitrary", "arbitrary")),
        name="gqa_attention",
    )(q, k, v, k, v)


def _rope_tables():
    t = jnp.arange(L)
    row = (t // GRID_W).astype(F32)
    col = (t % GRID_W).astype(F32)
    n_freq = ATT_DH // 4
    inv_freq = ROPE_THETA ** (-jnp.arange(n_freq, dtype=F32) / n_freq)
    ang = jnp.concatenate([row[:, None] * inv_freq, col[:, None] * inv_freq], axis=-1)
    cos, sin = jnp.cos(ang), jnp.sin(ang)
    cos_t = jnp.concatenate([cos, cos, cos, cos], axis=-1)
    sin_t = jnp.concatenate([-sin, sin, -sin, sin], axis=-1)
    cos_t = jnp.concatenate([cos_t, jnp.ones((TM, LANES), F32)], axis=0)
    sin_t = jnp.concatenate([sin_t, jnp.zeros((TM, LANES), F32)], axis=0)
    return cos_t, sin_t


def kernel(x, c, ctx, c_ctx, w_mod, b_mod, g_norm_mix, g_norm_ffn, w_ffn_up, w_ffn_conv, b_ffn_conv, w_ffn_down,
           w_fnet_out, w_na_qkv, na_rel_bias, w_na_out, w_sg_in, g_sg_v, w_sg_spatial, b_sg_spatial, w_sg_out,
           w_att_qkv, g_att_q, g_att_k, w_att_out, g_final):
    assert x.shape == (B, L, D) and ctx.shape == (B, NCTX, D) and w_mod.shape[0] == DEPTH == 4

    h = jnp.concatenate([x.reshape(T_LAT, D), ctx.reshape(T_CTX, D)], axis=0)
    cond8 = jnp.concatenate([c, c_ctx[None, :], jnp.zeros((8 - B - 1, D), F32)], axis=0)
    mods_all = _ada(cond8, w_mod, b_mod)

    w_up_bf = w_ffn_up.astype(BF16)
    w_down_bf = w_ffn_down.astype(BF16)

    def layer_mods(i):
        return mods_all[i, :B + 1].reshape((B + 1) * N_MOD, 1, D)

    def ffn(h, i, *, last):
        args = (layer_mods(i), g_norm_ffn[i][None, :], w_up_bf, w_ffn_conv, b_ffn_conv[:, None, :], w_down_bf,
                g_final[None, :])
        nl = T_LAT // TM
        if last:
            return _ffn(h, *args, layer=i, tm=TM, row0=0, n_compute=nl, n_blocks=nl, out_rows=T_LAT, in_place=False,
                        final_norm=True)
        h = _ffn(h, *args, layer=i, tm=NCTX, row0=T_LAT, n_compute=B, n_blocks=B, out_rows=T, in_place=True, final_norm=False)
        return _ffn(h, *args, layer=i, tm=TM, row0=0, n_compute=nl, n_blocks=T // TM, out_rows=T, in_place=False,
                    final_norm=False)

    mods = layer_mods(0)
    ang_c = _dft_angles(FNET_GD)
    fc = jnp.arange(FNET_GD, dtype=jnp.int32)
    a_c = ((fc[:, None] * fc[None, :]) % FNET_GD).astype(F32) * (2.0 * np.pi / FNET_GD)
    cs_chan = jnp.concatenate([jnp.cos(a_c), jnp.sin(a_c)], axis=1).astype(BF16)
    pq = _fnet_chan(h, mods, g_norm_mix[0][None, :], cs_chan)
    w_f = w_fnet_out[0].astype(BF16)
    h = _fnet_seq(h, pq, _dft_angles(L), mods, w_f, seq=L, row0=0, mod_row_fn=lambda b: b, name="fnet_seq_lat")
    h = _fnet_seq(h, pq, ang_c, mods, w_f, seq=NCTX, row0=T_LAT, mod_row_fn=lambda b: B, name="fnet_seq_ctx")
    h = ffn(h, 0, last=False)

    mods = layer_mods(1)
    scale = NA_DH ** -0.5 * LOG2E
    w_qkv = jnp.concatenate([w_na_qkv[0][:, :D] * scale, w_na_qkv[0][:, D:]], axis=1).astype(BF16)
    qkv = _modmm(h, mods, g_norm_mix[1][None, :], w_qkv, name="na_qkv")
    o_lat = _na_attend(qkv, _na_bias_table(na_rel_bias[0]))
    o_ctx = _na_ctx_attend(qkv)
    h = _outproj(h, o_lat, o_ctx, mods, w_na_out[0].astype(BF16), n_blocks=T // TM, name="na_out")
    h = ffn(h, 1, last=False)

    mods = layer_mods(2)
    b_s = jnp.broadcast_to(b_sg_spatial[0][:, :, None], (SG_GROUPS, SG_CHUNK, LANES))
    h = _sg(h, mods, g_norm_mix[2][None, :], w_sg_in[0].astype(BF16), g_sg_v[0][None, :],
            w_sg_spatial[0].astype(BF16), b_s, w_sg_out[0].astype(BF16))
    h = ffn(h, 2, last=False)

    mods = layer_mods(3)
    gq = jnp.tile(g_att_q[0] * (ATT_DH ** -0.5 * LOG2E), LANES // ATT_DH)[None, :]
    gk = jnp.tile(g_att_k[0], LANES // ATT_DH)[None, :]
    cos_t, sin_t = _rope_tables()
    q, k, v = _gqa_qkv(h, mods, g_norm_mix[3][None, :], w_att_qkv[0].astype(BF16), gq, gk, cos_t, sin_t)
    o = _gqa_attend(q, k, v)
    h = _outproj(h, o, o, mods, w_att_out[0].astype(BF16), n_blocks=T_LAT // TM, name="gqa_out")
    out = ffn(h, 3, last=True)
    return out.reshape(B, L, D)
```

```python
import functools

import numpy as np
import jax
import jax.numpy as jnp
from jax import lax
from jax.experimental import pallas as pl
from jax.experimental.pallas import tpu as pltpu

F32 = jnp.float32
BF16 = jnp.bfloat16

D = 1024
B = 4
L = 4096
NCTX = 256
DEPTH = 4
GRID_W = 64
EPS = 1e-6
N_MOD = 6
T_LAT = B * L
T_CTX = B * NCTX
T = T_LAT + T_CTX

FNET_GROUPS = 4
FNET_GD = D // FNET_GROUPS

NA_HEADS = 16
NA_DH = D // NA_HEADS
NA_WIN_ROWS = 8
NA_WIN_COLS = 16
NA_QROWS = 4
NA_KROWS = 12
NA_TQ = NA_QROWS * GRID_W
NA_TK = NA_KROWS * GRID_W
NA_HB = 8

SG_CHUNK = 128
SG_GROUPS = 4
SG_GD = D // SG_GROUPS

ATT_HEADS = 16
ATT_KV = 4
ATT_DH = 64
ROPE_THETA = 10000.0
GQ_TQ = 256
GQ_CK = 256

FFN_DIM = 2816
FFN_TF = 256
FFN_NF = FFN_DIM // FFN_TF
HALO = 16

TM = 1024
SG_TM = 512
LANES = 128
NEG = -1e30
LOG2E = float(np.log2(np.e))

VMEM_LIMIT = 56 * 1024 * 1024


def _cp(sem, vmem=VMEM_LIMIT):
    return pltpu.CompilerParams(dimension_semantics=sem, vmem_limit_bytes=vmem)


def _mod_row(i, tm):
    return (i * tm) // L


def _mod_spec(k, tm, b0=0):
    return pl.BlockSpec((None, 1, D), lambda i: (_mod_row(b0 + i, tm) * N_MOD + k, 0, 0))


def _const_spec(shape, single_buffer=False):
    nd = len(shape)
    if single_buffer:
        return pl.BlockSpec(shape, lambda *_: (0,) * nd, pipeline_mode=pl.Buffered(1))
    return pl.BlockSpec(shape, lambda *_: (0,) * nd)


def _layer_spec(shape, layer, single_buffer=False):
    nd = len(shape)
    mode = dict(pipeline_mode=pl.Buffered(1)) if single_buffer else {}
    return pl.BlockSpec((None,) + tuple(shape), lambda *_: (layer,) + (0,) * nd, **mode)


def _modulate(x, g, shift, scale):
    ms = jnp.mean(x * x, axis=-1, keepdims=True)
    return x * lax.rsqrt(ms + EPS) * (g * (1.0 + scale)) + shift


def _dot(a, b):
    return jnp.dot(a, b, preferred_element_type=F32)


def _dot_nt(a, b):
    return lax.dot_general(a, b, (((1,), (1,)), ((), ())), preferred_element_type=F32)


def _ada_kernel(c_ref, w_ref, b_ref, o_ref):
    c = c_ref[...]
    s = (c * jax.nn.sigmoid(c)).astype(BF16)
    o_ref[0] = _dot(s, w_ref[0].astype(BF16)) + b_ref[0]


def _ada(cond8, w_mod, b_mod):
    tn = 2048
    n = N_MOD * D
    return pl.pallas_call(
        _ada_kernel,
        out_shape=jax.ShapeDtypeStruct((DEPTH, 8, n), F32),
        grid=(DEPTH, n // tn),
        in_specs=[
            pl.BlockSpec((8, D), lambda l, j: (0, 0)),
            pl.BlockSpec((1, D, tn), lambda l, j: (l, 0, j)),
            pl.BlockSpec((1, 1, tn), lambda l, j: (l, 0, j)),
        ],
        out_specs=pl.BlockSpec((1, 8, tn), lambda l, j: (l, 0, j)),
        compiler_params=_cp(("arbitrary", "arbitrary")),
        name="ada_params",
    )(cond8, w_mod, b_mod.reshape(DEPTH, 1, n))


def _ffn_kernel(hp_ref, h_ref, hn_ref, g_ref, sh_ref, sc_ref, gt_ref, wu_ref, wc_ref, bc_ref,
                wd_ref, gf_ref, o_ref, a_ref, gb_ref, vb_ref, *, tm, blocks_per_seq, n_compute, final_norm):
    i = pl.program_id(0)

    @pl.when(i >= n_compute)
    def _():
        o_ref[...] = h_ref[...]

    @pl.when(i < n_compute)
    def _():
        _ffn_block(hp_ref, h_ref, hn_ref, g_ref, sh_ref, sc_ref, gt_ref, wu_ref, wc_ref, bc_ref, wd_ref,
                   gf_ref, o_ref, a_ref, gb_ref, vb_ref, tm=tm, blocks_per_seq=blocks_per_seq, final_norm=final_norm)


def _ffn_block(hp_ref, h_ref, hn_ref, g_ref, sh_ref, sc_ref, gt_ref, wu_ref, wc_ref, bc_ref, wd_ref,
               gf_ref, o_ref, a_ref, gb_ref, vb_ref, *, tm, blocks_per_seq, final_norm):
    i = pl.program_id(0)
    g = g_ref[...]
    sh = sh_ref[...]
    sc = sc_ref[...]

    if blocks_per_seq == 1:
        a_ref[0:HALO, :] = jnp.zeros((HALO, D), BF16)
        a_ref[HALO + tm:, :] = jnp.zeros((HALO, D), BF16)
    else:
        seq_first = (i % blocks_per_seq) == 0
        seq_last = (i % blocks_per_seq) == blocks_per_seq - 1
        a_ref[0:HALO, :] = jnp.where(seq_first, 0.0, _modulate(hp_ref[...], g, sh, sc)).astype(BF16)
        a_ref[HALO + tm:, :] = jnp.where(seq_last, 0.0, _modulate(hn_ref[...], g, sh, sc)).astype(BF16)

    def up(j, r0=0, r1=tm):
        c0 = j * FFN_TF
        e0 = 0 if r0 == 0 else HALO + r0
        e1 = tm + 2 * HALO if r1 == tm else HALO + r1
        gb_ref[j % 2, e0:e1, :] = _dot(a_ref[e0:e1, :], wu_ref[:, c0:c0 + FFN_TF])
        vb_ref[j % 2, r0:r1, :] = _dot(a_ref[HALO + r0:HALO + r1, :],
                                       wu_ref[:, FFN_DIM + c0:FFN_DIM + c0 + FFN_TF])

    n_piece = 4 if tm == TM else 1
    pieces = [(p * tm // n_piece, (p + 1) * tm // n_piece) for p in range(n_piece)]
    for r0, r1 in pieces:
        a_ref[HALO + r0:HALO + r1, :] = _modulate(h_ref[r0:r1, :], g, sh, sc).astype(BF16)
        up(0, r0, r1)
    for j in range(FFN_NF):
        if j + 1 < FFN_NF:
            up(j + 1)
        gb = gb_ref.at[j % 2]
        c0 = j * FFN_TF
        wc = wc_ref[:, c0:c0 + FFN_TF]
        gc = (gb[HALO:HALO + tm, :] * wc[1:2]
              + gb[HALO - 1:HALO - 1 + tm, :] * wc[0:1]
              + gb[HALO + 1:HALO + 1 + tm, :] * wc[2:3]
              + bc_ref[:, c0:c0 + FFN_TF])
        u = (gc * jax.nn.sigmoid(gc) * vb_ref[j % 2]).astype(BF16)
        if j + 1 < FFN_NF:
            y = _dot(u, wd_ref[c0:c0 + FFN_TF, :])
            if j == 0:
                o_ref[...] = y
            else:
                o_ref[...] += y
        else:
            for r0, r1 in pieces:
                acc = o_ref[r0:r1, :] + _dot(u[r0:r1, :], wd_ref[c0:c0 + FFN_TF, :])
                out = h_ref[r0:r1, :] + gt_ref[...] * acc
                if final_norm:
                    ms = jnp.mean(out * out, axis=-1, keepdims=True)
                    out = out * lax.rsqrt(ms + EPS) * gf_ref[...]
                o_ref[r0:r1, :] = out


def _ffn(h, mods, g_norm, wu, wc, bc, wd, g_final, *, layer, tm, row0, n_compute, n_blocks, out_rows, in_place,
         final_norm):
    hb = tm // HALO
    b0 = row0 // tm
    last_halo = T // HALO - 1
    blocks_per_seq = L // tm if row0 == 0 else NCTX // tm
    assert not (in_place and blocks_per_seq != 1)
    kern = functools.partial(_ffn_kernel, tm=tm, blocks_per_seq=blocks_per_seq, n_compute=n_compute,
                             final_norm=final_norm)
    if blocks_per_seq == 1:
        halo = jnp.zeros((HALO, D), F32)
        halo_specs = [_const_spec((HALO, D)), _const_spec((HALO, D))]
    else:
        halo = h
        halo_specs = [pl.BlockSpec((HALO, D), lambda i: (jnp.maximum((b0 + i) * hb - 1, 0), 0)),
                      pl.BlockSpec((HALO, D), lambda i: (jnp.minimum((b0 + i + 1) * hb, last_halo), 0))]
    in_specs = [
        halo_specs[0],
        pl.BlockSpec((tm, D), lambda i: (b0 + i, 0)),
        halo_specs[1],
        _const_spec((1, D)),
        _mod_spec(3, tm, b0), _mod_spec(4, tm, b0), _mod_spec(5, tm, b0),
        _layer_spec((D, 2 * FFN_DIM), layer, single_buffer=True),
        _layer_spec((3, FFN_DIM), layer),
        _layer_spec((1, FFN_DIM), layer),
        _layer_spec((FFN_DIM, D), layer, single_buffer=True),
        _const_spec((1, D)),
    ]
    return pl.pallas_call(
        kern,
        out_shape=jax.ShapeDtypeStruct((out_rows, D), F32),
        grid=(n_blocks,),
        in_specs=in_specs,
        out_specs=pl.BlockSpec((tm, D), lambda i: (b0 + i, 0)),
        scratch_shapes=[pltpu.VMEM((tm + 2 * HALO, D), BF16), pltpu.VMEM((2, tm + 2 * HALO, FFN_TF), F32),
                        pltpu.VMEM((2, tm, FFN_TF), F32)],
        input_output_aliases={1: 0} if in_place else {},
        compiler_params=_cp(("arbitrary",)),
        name="conv_ffn_lat" if row0 == 0 else "conv_ffn_ctx",
    )(halo, h, halo, g_norm, mods, mods, mods, wu, wc, bc, wd, g_final)


def _modmm_kernel(h_ref, g_ref, sh_ref, sc_ref, w_ref, o_ref, *, tn):
    a = _modulate(h_ref[...], g_ref[...], sh_ref[...], sc_ref[...]).astype(BF16)
    for c in range(w_ref.shape[1] // tn):
        o_ref[:, c * tn:(c + 1) * tn] = _dot(a, w_ref[:, c * tn:(c + 1) * tn]).astype(BF16)


def _modmm(h, mods, g_norm, w, *, name):
    n = w.shape[1]
    return pl.pallas_call(
        functools.partial(_modmm_kernel, tn=512),
        out_shape=jax.ShapeDtypeStruct((T, n), BF16),
        grid=(T // TM,),
        in_specs=[
            pl.BlockSpec((TM, D), lambda i: (i, 0)),
            _const_spec((1, D)),
            _mod_spec(0, TM), _mod_spec(1, TM),
            _const_spec((D, n)),
        ],
        out_specs=pl.BlockSpec((TM, n), lambda i: (i, 0)),
        compiler_params=_cp(("arbitrary",)),
        name=name,
    )(h, g_norm, mods, mods, w)


def _outproj_kernel(h_ref, xl_ref, xc_ref, gt_ref, w_ref, o_ref, *, n_lat):
    i = pl.program_id(0)

    @pl.when(i < n_lat)
    def _():
        o_ref[...] = h_ref[...] + gt_ref[...] * _dot(xl_ref[...], w_ref[...])

    @pl.when(i >= n_lat)
    def _():
        o_ref[...] = h_ref[...] + gt_ref[...] * _dot(xc_ref[...], w_ref[...])


def _outproj(h, x_lat, x_ctx, mods, w, *, n_blocks, name):
    n_lat = T_LAT // TM
    return pl.pallas_call(
        functools.partial(_outproj_kernel, n_lat=n_lat),
        out_shape=jax.ShapeDtypeStruct((T, D), F32),
        grid=(n_blocks,),
        in_specs=[
            pl.BlockSpec((TM, D), lambda i: (i, 0)),
            pl.BlockSpec((TM, D), lambda i: (jnp.minimum(i, n_lat - 1), 0)),
            pl.BlockSpec((TM, D), lambda i: (jnp.maximum(i - n_lat, 0), 0)),
            _mod_spec(2, TM),
            _const_spec((D, D)),
        ],
        out_specs=pl.BlockSpec((TM, D), lambda i: (i, 0)),
        input_output_aliases={0: 0},
        compiler_params=_cp(("arbitrary",)),
        name=name,
    )(h, x_lat, x_ctx, mods, w)


def _fnet_chan_kernel(x_ref, c_ref, g_ref, sh_ref, sc_ref, cs_ref, o_ref, h_ref, *, n_lat):
    i = pl.program_id(0)

    @pl.when(i < n_lat)
    def _():
        h_ref[...] = x_ref[...]

    @pl.when(i >= n_lat)
    def _():
        h_ref[...] = c_ref[...]

    a = _modulate(h_ref[...], g_ref[...], sh_ref[...], sc_ref[...]).astype(BF16)
    cs = cs_ref[...]
    for grp in range(FNET_GROUPS):
        pq = _dot(a[:, grp * FNET_GD:(grp + 1) * FNET_GD], cs)
        o_ref[0, :, grp * FNET_GD:(grp + 1) * FNET_GD] = pq[:, :FNET_GD].astype(BF16)
        o_ref[1, :, grp * FNET_GD:(grp + 1) * FNET_GD] = pq[:, FNET_GD:].astype(BF16)


def _fnet_chan(x_flat, ctx_flat, mods, g_norm, cs_chan):
    n_lat = T_LAT // TM
    assert T_CTX == TM
    return pl.pallas_call(
        functools.partial(_fnet_chan_kernel, n_lat=n_lat),
        out_shape=(jax.ShapeDtypeStruct((2, T, D), BF16), jax.ShapeDtypeStruct((T, D), F32)),
        grid=(T // TM,),
        in_specs=[
            pl.BlockSpec((TM, D), lambda i: (jnp.minimum(i, n_lat - 1), 0)),
            _const_spec((TM, D)),
            _const_spec((1, D)),
            _mod_spec(0, TM), _mod_spec(1, TM),
            _const_spec((FNET_GD, 2 * FNET_GD)),
        ],
        out_specs=(pl.BlockSpec((2, TM, D), lambda i: (0, i, 0)), pl.BlockSpec((TM, D), lambda i: (i, 0))),
        compiler_params=_cp(("arbitrary",)),
        name="fnet_channel_dft",
    )(x_flat, ctx_flat, g_norm, mods, mods, cs_chan)


def _fnet_seq_kernel(hi_ref, lo_ref, pq_ref, h_ref, gt_ref, w_ref, o_ref, acc_ref, tab_ref, *, r, nkk, inv_norm):
    k = pl.program_id(2)
    use_cos = k < nkk
    c1, s1 = hi_ref[0], hi_ref[1]
    c0, s0 = lo_ref[0], lo_ref[1]
    a1 = jnp.where(use_cos, c1, -s1)
    b1 = jnp.where(use_cos, -s1, -c1)
    for fh in range(c1.shape[0]):
        tab_ref[fh * r:(fh + 1) * r, :] = (a1[fh:fh + 1, :] * c0 + b1[fh:fh + 1, :] * s0).astype(BF16)
    y = _dot(tab_ref[...], pq_ref[0])

    @pl.when(k == 0)
    def _():
        acc_ref[...] = y

    @pl.when((k > 0) & (k < 2 * nkk - 1))
    def _():
        acc_ref[...] += y

    @pl.when(k == 2 * nkk - 1)
    def _():
        f = ((acc_ref[...] + y) * inv_norm).astype(BF16)
        o_ref[...] = h_ref[...] + gt_ref[...] * _dot(f, w_ref[...])


def _fnet_seq(h, pq, angles, mods, w, *, seq, row0, mod_row_fn, name):
    hi, lo = angles
    r = lo.shape[1]
    tm = min(TM, seq)
    tk = min(2048, seq)
    nkk = seq // tk
    mt = seq // tm
    rb0 = row0 // tm
    kb0 = row0 // tk
    kern = functools.partial(_fnet_seq_kernel, r=r, nkk=nkk, inv_norm=1.0 / np.sqrt(seq * FNET_GD))
    return pl.pallas_call(
        kern,
        out_shape=jax.ShapeDtypeStruct((T, D), F32),
        grid=(B, mt, 2 * nkk),
        in_specs=[
            pl.BlockSpec((2, tm // r, tk), lambda b, m, k: (0, m, k % nkk)),
            pl.BlockSpec((2, r, tk), lambda b, m, k: (0, 0, k % nkk)),
            pl.BlockSpec((1, tk, D), lambda b, m, k: (k // nkk, kb0 + b * nkk + k % nkk, 0)),
            pl.BlockSpec((tm, D), lambda b, m, k: (rb0 + b * mt + m, 0)),
            pl.BlockSpec((None, 1, D), lambda b, m, k: (mod_row_fn(b) * N_MOD + 2, 0, 0)),
            _const_spec((D, D)),
        ],
        out_specs=pl.BlockSpec((tm, D), lambda b, m, k: (rb0 + b * mt + m, 0)),
        scratch_shapes=[pltpu.VMEM((tm, D), F32), pltpu.VMEM((tm, tk), BF16)],
        input_output_aliases={3: 0},
        compiler_params=_cp(("arbitrary", "arbitrary", "arbitrary")),
        name=name,
    )(hi, lo, pq, h, mods, w)


def _dft_angles(n):
    r = int(np.sqrt(n))
    assert r * r == n
    t = jnp.arange(n, dtype=jnp.int32)[None, :]
    f = jnp.arange(r, dtype=jnp.int32)[:, None]
    a_hi = ((f * t) % r).astype(F32) * (2.0 * np.pi / r)
    a_lo = ((f * t) % n).astype(F32) * (2.0 * np.pi / n)
    return jnp.stack([jnp.cos(a_hi), jnp.sin(a_hi)]), jnp.stack([jnp.cos(a_lo), jnp.sin(a_lo)])


def _na_bias_table(rpb):
    n_co = 2 * NA_WIN_COLS - 1
    qc = np.arange(GRID_W)[:, None]
    kc = np.arange(GRID_W)[None, :]
    cs = np.clip(qc - NA_WIN_COLS // 2, 0, GRID_W - NA_WIN_COLS)
    col_ok = (kc >= cs) & (kc < cs + NA_WIN_COLS)
    col_sel = (kc - qc + NA_WIN_COLS - 1)[None] == np.arange(n_co)[:, None, None]
    by_col = jnp.einsum("hrd,dqk->hrqk", rpb * LOG2E, jnp.asarray(col_sel, F32), precision=lax.Precision.HIGHEST)
    a = jnp.where(col_ok, by_col, NEG)
    z = jnp.zeros((NA_HEADS, 1, GRID_W, GRID_W), F32)
    return jnp.concatenate([jnp.concatenate([z, a], axis=1), jnp.concatenate([a, z], axis=1)], axis=-1)


def _na_row_tables():
    rows = L // GRID_W
    nq = L // NA_TQ
    npair = NA_KROWS // 2
    idx = np.zeros((nq, NA_QROWS * npair), np.int32)
    rmask = np.zeros((nq, NA_QROWS * npair, LANES), np.float32)
    for g in range(nq):
        k0 = int(np.clip(NA_QROWS * g - NA_WIN_ROWS // 2, 0, rows - NA_KROWS))
        for qi in range(NA_QROWS):
            r = NA_QROWS * g + qi
            rs = int(np.clip(r - NA_WIN_ROWS // 2, 0, rows - NA_WIN_ROWS))
            for m in range(npair):
                kr = k0 + 2 * m
                idx[g, qi * npair + m] = int(np.clip(kr - r + NA_WIN_ROWS, 0, 2 * NA_WIN_ROWS - 1))
                for half in range(2):
                    if not rs <= kr + half < rs + NA_WIN_ROWS:
                        rmask[g, qi * npair + m, half * NA_DH:(half + 1) * NA_DH] = NEG
    return idx, rmask


def _na_kernel(idx_ref, q_ref, k_ref, v_ref, kc_ref, vc_ref, tab_ref, rmask_ref, o_ref, snb_ref, scx_ref):
    g = pl.program_id(2)
    k0 = jnp.clip(NA_QROWS * g - NA_WIN_ROWS // 2, 0, L // GRID_W - NA_KROWS)
    start = pl.multiple_of(k0 * GRID_W, GRID_W)
    lane = lax.broadcasted_iota(jnp.int32, (1, LANES), 1)
    lower = lane < NA_DH
    npair = NA_KROWS // 2

    heads = [(t, a) for t in range(NA_HB) for a in range(2)]

    def scores(i):
        t, a = heads[i]
        cols = slice(t * LANES, (t + 1) * LANES)
        q = q_ref[:, cols]
        qm = jnp.where((lane // NA_DH) == a, q, jnp.zeros_like(q))
        bias = jnp.concatenate(
            [jnp.concatenate([tab_ref[2 * t + a, idx_ref[g, qi * npair + m]]
                              + rmask_ref[qi * npair + m:qi * npair + m + 1, :] for m in range(npair)], axis=1)
             for qi in range(NA_QROWS)], axis=0)
        snb_ref[i % 2] = _dot_nt(qm, k_ref[pl.ds(start, NA_TK), cols]) + bias
        scx_ref[i % 2] = _dot_nt(qm, kc_ref[:, cols])

    scores(0)
    prev = None
    for i, (t, a) in enumerate(heads):
        if i + 1 < len(heads):
            scores(i + 1)
        cols = slice(t * LANES, (t + 1) * LANES)
        sel = (lane // NA_DH) == a
        s_nb = snb_ref[i % 2]
        s_cx = scx_ref[i % 2]
        mx = jnp.maximum(jnp.max(s_nb, axis=-1, keepdims=True), jnp.max(s_cx, axis=-1, keepdims=True))
        vw = v_ref[pl.ds(start, NA_TK), cols]
        vc = vc_ref[:, cols]
        o = (_dot(jnp.exp2(s_nb - mx).astype(BF16), jnp.where(sel, vw, jnp.ones_like(vw)))
             + _dot(jnp.exp2(s_cx - mx).astype(BF16), jnp.where(sel, vc, jnp.ones_like(vc))))
        o = o * (1.0 / o[:, (1 - a) * NA_DH:(1 - a) * NA_DH + 1])
        if a == 0:
            prev = o
        else:
            o_ref[:, cols] = jnp.where(lower, prev, o).astype(BF16)


def _na_attend(qkv, table):
    idx, rmask = _na_row_tables()
    wb = NA_HB * LANES
    nh = D // wb
    nq = L // NA_TQ
    return pl.pallas_call(
        _na_kernel,
        out_shape=jax.ShapeDtypeStruct((T_LAT, D), BF16),
        grid=(B, nh, nq),
        in_specs=[
            pl.BlockSpec(memory_space=pltpu.SMEM),
            pl.BlockSpec((NA_TQ, wb), lambda b, h, g: (b * nq + g, h)),
            pl.BlockSpec((L, wb), lambda b, h, g: (b, nh + h)),
            pl.BlockSpec((L, wb), lambda b, h, g: (b, 2 * nh + h)),
            pl.BlockSpec((NCTX, wb), lambda b, h, g: (T_LAT // NCTX + b, nh + h)),
            pl.BlockSpec((NCTX, wb), lambda b, h, g: (T_LAT // NCTX + b, 2 * nh + h)),
            pl.BlockSpec((2 * NA_HB, 2 * NA_WIN_ROWS, GRID_W, LANES), lambda b, h, g: (h, 0, 0, 0),
                         pipeline_mode=pl.Buffered(1)),
            pl.BlockSpec((None, NA_QROWS * NA_KROWS // 2, LANES), lambda b, h, g: (g, 0, 0)),
        ],
        out_specs=pl.BlockSpec((NA_TQ, wb), lambda b, h, g: (b * nq + g, h)),
        scratch_shapes=[pltpu.VMEM((2, NA_TQ, NA_TK), F32), pltpu.VMEM((2, NA_TQ, NCTX), F32)],
        compiler_params=_cp(("arbitrary", "arbitrary", "arbitrary")),
        name="na_attention",
    )(jnp.asarray(idx), qkv, qkv, qkv, qkv, qkv, table, jnp.asarray(rmask))


def _ctx_attn_kernel(q_ref, kc_ref, vc_ref, o_ref):
    lane = lax.broadcasted_iota(jnp.int32, (1, LANES), 1)
    for t in range(D // LANES):
        cols = slice(t * LANES, (t + 1) * LANES)
        kc = kc_ref[:, cols]
        vc = vc_ref[:, cols]
        q = q_ref[:, cols]
        outs = []
        for a in range(2):
            qm = jnp.where((lane // NA_DH) == a, q, jnp.zeros_like(q))
            s = _dot_nt(qm, kc)
            p = jnp.exp2(s - jnp.max(s, axis=-1, keepdims=True))
            outs.append(_dot(p.astype(BF16), vc) / jnp.sum(p, axis=-1, keepdims=True))
        o_ref[:, cols] = jnp.where(lane < NA_DH, outs[0], outs[1]).astype(BF16)


def _na_ctx_attend(qkv):
    cb = T_LAT // NCTX
    return pl.pallas_call(
        _ctx_attn_kernel,
        out_shape=jax.ShapeDtypeStruct((T_CTX, D), BF16),
        grid=(B,),
        in_specs=[
            pl.BlockSpec((NCTX, D), lambda b: (cb + b, 0)),
            pl.BlockSpec((NCTX, D), lambda b: (cb + b, 1)),
            pl.BlockSpec((NCTX, D), lambda b: (cb + b, 2)),
        ],
        out_specs=pl.BlockSpec((NCTX, D), lambda b: (b, 0)),
        compiler_params=_cp(("arbitrary",)),
        name="na_ctx_attention",
    )(qkv, qkv, qkv)


def _sg_kernel(h_ref, g_ref, sh_ref, sc_ref, gt_ref, win_ref, gv_ref, ws_ref, bs_ref, wout_ref, o_ref,
               u_ref, v_ref, t_ref):
    h = h_ref[...]
    a = _modulate(h, g_ref[...], sh_ref[...], sc_ref[...]).astype(BF16)
    tn = 512
    ssq = jnp.zeros((SG_TM, 1), F32)
    for c in range(2 * D // tn):
        z = jax.nn.gelu(_dot(a, win_ref[:, c * tn:(c + 1) * tn]), approximate=True)
        if c * tn < D:
            u_ref[:, c * tn:(c + 1) * tn] = z
        else:
            v_ref[:, c * tn - D:(c + 1) * tn - D] = z
            ssq = ssq + jnp.sum(z * z, axis=-1, keepdims=True)
    scale = lax.rsqrt(ssq * (1.0 / D) + EPS)
    vn = (v_ref[...] * scale * gv_ref[...]).astype(BF16)
    for ch in range(SG_TM // SG_CHUNK):
        r0 = ch * SG_CHUNK
        for grp in range(SG_GROUPS):
            c0 = grp * SG_GD
            mixed = _dot(ws_ref[grp], vn[r0:r0 + SG_CHUNK, c0:c0 + SG_GD])
            bs = bs_ref[grp]
            mixed = mixed + jnp.concatenate([bs, bs], axis=-1)
            t_ref[r0:r0 + SG_CHUNK, c0:c0 + SG_GD] = (u_ref[r0:r0 + SG_CHUNK, c0:c0 + SG_GD] * mixed).astype(BF16)
    o_ref[...] = h + gt_ref[...] * _dot(t_ref[...], wout_ref[...])


def _sg(h, mods, g_norm, w_in, g_v, w_s, b_s, w_out):
    return pl.pallas_call(
        _sg_kernel,
        out_shape=jax.ShapeDtypeStruct((T, D), F32),
        grid=(T // SG_TM,),
        in_specs=[
            pl.BlockSpec((SG_TM, D), lambda i: (i, 0)),
            _const_spec((1, D)),
            _mod_spec(0, SG_TM), _mod_spec(1, SG_TM), _mod_spec(2, SG_TM),
            _const_spec((D, 2 * D)),
            _const_spec((1, D)),
            _const_spec((SG_GROUPS, SG_CHUNK, SG_CHUNK)),
            _const_spec((SG_GROUPS, SG_CHUNK, LANES)),
            _const_spec((D, D)),
        ],
        out_specs=pl.BlockSpec((SG_TM, D), lambda i: (i, 0)),
        scratch_shapes=[pltpu.VMEM((SG_TM, D), F32), pltpu.VMEM((SG_TM, D), F32), pltpu.VMEM((SG_TM, D), BF16)],
        compiler_params=_cp(("arbitrary",)),
        name="spatial_gating",
    )(h, g_norm, mods, mods, mods, w_in, g_v, w_s, b_s, w_out)


def _gqa_qkv_kernel(h_ref, g_ref, sh_ref, sc_ref, w_ref, gq_ref, gk_ref, cos_ref, sin_ref,
                    q_ref, k_ref, v_ref, y_ref):
    a = _modulate(h_ref[...], g_ref[...], sh_ref[...], sc_ref[...]).astype(BF16)
    cos = cos_ref[...]
    sin = sin_ref[...]
    lane = lax.broadcasted_iota(jnp.int32, (1, LANES), 1)
    head_lanes = lane < ATT_DH
    first_half = (lane % ATT_DH) < (ATT_DH // 2)
    tn = 256
    ri = lax.broadcasted_iota(jnp.int32, (tn, tn), 0) // ATT_DH
    ci = lax.broadcasted_iota(jnp.int32, (tn, tn), 1) // ATT_DH
    head_mean = jnp.where(ri == ci, 1.0 / ATT_DH, 0.0).astype(BF16)
    n_qk = (ATT_HEADS + ATT_KV) * ATT_DH
    n_q = ATT_HEADS * ATT_DH
    n_chunk = w_ref.shape[1] // tn

    def project(c):
        y_ref[c % 2] = _dot(a, w_ref[:, c * tn:(c + 1) * tn])

    project(0)
    for c in range(n_chunk):
        if c + 1 < n_chunk:
            project(c + 1)
        y2 = y_ref[c % 2]
        if c * tn < n_qk:
            ms2 = _dot((y2 * y2).astype(BF16), head_mean)
        for s in range(tn // LANES):
            col = c * tn + s * LANES
            y = y2[:, s * LANES:(s + 1) * LANES]
            if col < n_qk:
                gain = gq_ref[...] if col < n_q else gk_ref[...]
                yn = y * lax.rsqrt(ms2[:, s * LANES:(s + 1) * LANES] + EPS) * gain
                partner = jnp.where(first_half, pltpu.roll(yn, LANES - ATT_DH // 2, 1), pltpu.roll(yn, ATT_DH // 2, 1))
                y = yn * cos + partner * sin
            if col < n_q:
                dst, base, fill = q_ref, col, 0.0
            elif col < n_qk:
                dst, base, fill = k_ref, col - n_q, 0.0
            else:
                dst, base, fill = v_ref, col - n_qk, 1.0
            dst[:, 2 * base:2 * base + LANES] = jnp.where(head_lanes, y, fill).astype(BF16)
            dst[:, 2 * base + LANES:2 * base + 2 * LANES] = jnp.where(
                head_lanes, pltpu.roll(y, ATT_DH, 1), fill).astype(BF16)


def _gqa_qkv(h, mods, g_norm, w, gq, gk, cos_t, sin_t):
    tab_spec = pl.BlockSpec((TM, LANES), lambda i: (jnp.where(i * TM >= T_LAT, L // TM, i % (L // TM)), 0))
    nq, nkv = ATT_HEADS * LANES, ATT_KV * LANES
    return pl.pallas_call(
        _gqa_qkv_kernel,
        out_shape=(jax.ShapeDtypeStruct((T, nq), BF16), jax.ShapeDtypeStruct((T, nkv), BF16),
                   jax.ShapeDtypeStruct((T, nkv), BF16)),
        grid=(T // TM,),
        in_specs=[
            pl.BlockSpec((TM, D), lambda i: (i, 0)),
            _const_spec((1, D)),
            _mod_spec(0, TM), _mod_spec(1, TM),
            _const_spec(w.shape),
            _const_spec((1, LANES)), _const_spec((1, LANES)),
            tab_spec, tab_spec,
        ],
        out_specs=(pl.BlockSpec((TM, nq), lambda i: (i, 0)), pl.BlockSpec((TM, nkv), lambda i: (i, 0)),
                   pl.BlockSpec((TM, nkv), lambda i: (i, 0))),
        scratch_shapes=[pltpu.VMEM((2, TM, 256), F32)],
        compiler_params=_cp(("arbitrary",)),
        name="gqa_qkv_rope",
    )(h, g_norm, mods, mods, w, gq, gk, cos_t, sin_t)


def _gqa_kernel(q_ref, k_ref, v_ref, kc_ref, vc_ref, o_ref):
    lane = lax.broadcasted_iota(jnp.int32, (1, LANES), 1)
    nh = ATT_HEADS // ATT_KV
    qs = jnp.concatenate([q_ref[:, i * LANES:(i + 1) * LANES] for i in range(nh)], axis=0)
    chunks = [(k_ref, v_ref, c * GQ_CK, GQ_CK) for c in range(L // GQ_CK)] + [(kc_ref, vc_ref, 0, NCTX)]
    m = jnp.full((nh * GQ_TQ, 1), -jnp.inf, F32)
    acc = jnp.zeros((nh * GQ_TQ, LANES), F32)
    for kr, vr, c0, cn in chunks:
        s = _dot_nt(qs, kr[c0:c0 + cn, :])
        m_new = jnp.maximum(m, jnp.max(s, axis=-1, keepdims=True))
        p = jnp.exp2(s - m_new).astype(BF16)
        acc = jnp.exp2(m - m_new) * acc + _dot(p, vr[c0:c0 + cn, :])
        m = m_new
    o = acc * (1.0 / acc[:, ATT_DH:ATT_DH + 1])
    for j in range(nh // 2):
        oa = o[2 * j * GQ_TQ:(2 * j + 1) * GQ_TQ]
        ob = o[(2 * j + 1) * GQ_TQ:(2 * j + 2) * GQ_TQ]
        o_ref[:, j * LANES:(j + 1) * LANES] = jnp.where(lane < ATT_DH, oa, pltpu.roll(ob, ATT_DH, 1)).astype(BF16)


def _gqa_attend(q, k, v):
    nh = ATT_HEADS // ATT_KV
    nqt = L // GQ_TQ
    cb = T_LAT // NCTX
    return pl.pallas_call(
        _gqa_kernel,
        out_shape=jax.ShapeDtypeStruct((T_LAT, D), BF16),
        grid=(B, ATT_KV, nqt),
        in_specs=[
            pl.BlockSpec((GQ_TQ, nh * LANES), lambda b, h, t: (b * nqt + t, h)),
            pl.BlockSpec((L, LANES), lambda b, h, t: (b, h)),
            pl.BlockSpec((L, LANES), lambda b, h, t: (b, h)),
            pl.BlockSpec((NCTX, LANES), lambda b, h, t: (cb + b, h)),
            pl.BlockSpec((NCTX, LANES), lambda b, h, t: (cb + b, h)),
        ],
        out_specs=pl.BlockSpec((GQ_TQ, nh * ATT_DH), lambda b, h, t: (b * nqt + t, h)),
        compiler_params=_cp(("arbitrary", "arbitrary", "arbitrary")),
        name="gqa_attention",
    )(q, k, v, k, v)


def _rope_tables():
    t = jnp.arange(L)
    row = (t // GRID_W).astype(F32)
    col = (t % GRID_W).astype(F32)
    n_freq = ATT_DH // 4
    inv_freq = ROPE_THETA ** (-jnp.arange(n_freq, dtype=F32) / n_freq)
    ang = jnp.concatenate([row[:, None] * inv_freq, col[:, None] * inv_freq], axis=-1)
    cos, sin = jnp.cos(ang), jnp.sin(ang)
    cos_t = jnp.concatenate([cos, cos, cos, cos], axis=-1)
    sin_t = jnp.concatenate([-sin, sin, -sin, sin], axis=-1)
    cos_t = jnp.concatenate([cos_t, jnp.ones((TM, LANES), F32)], axis=0)
    sin_t = jnp.concatenate([sin_t, jnp.zeros((TM, LANES), F32)], axis=0)
    return cos_t, sin_t


def kernel(x, c, ctx, c_ctx, w_mod, b_mod, g_norm_mix, g_norm_ffn, w_ffn_up, w_ffn_conv, b_ffn_conv, w_ffn_down,
           w_fnet_out, w_na_qkv, na_rel_bias, w_na_out, w_sg_in, g_sg_v, w_sg_spatial, b_sg_spatial, w_sg_out,
           w_att_qkv, g_att_q, g_att_k, w_att_out, g_final):
    assert x.shape == (B, L, D) and ctx.shape == (B, NCTX, D) and w_mod.shape[0] == DEPTH == 4

    cond8 = jnp.concatenate([c, c_ctx[None, :], jnp.zeros((8 - B - 1, D), F32)], axis=0)
    mods_all = _ada(cond8, w_mod, b_mod)

    w_up_bf = w_ffn_up.astype(BF16)
    w_down_bf = w_ffn_down.astype(BF16)

    def layer_mods(i):
        return mods_all[i, :B + 1].reshape((B + 1) * N_MOD, 1, D)

    def ffn(h, i, *, last):
        args = (layer_mods(i), g_norm_ffn[i][None, :], w_up_bf, w_ffn_conv, b_ffn_conv[:, None, :], w_down_bf,
                g_final[None, :])
        nl = T_LAT // TM
        if last:
            return _ffn(h, *args, layer=i, tm=TM, row0=0, n_compute=nl, n_blocks=nl, out_rows=T_LAT, in_place=False,
                        final_norm=True)
        h = _ffn(h, *args, layer=i, tm=NCTX, row0=T_LAT, n_compute=B, n_blocks=B, out_rows=T, in_place=True, final_norm=False)
        return _ffn(h, *args, layer=i, tm=TM, row0=0, n_compute=nl, n_blocks=T // TM, out_rows=T, in_place=False,
                    final_norm=False)

    mods = layer_mods(0)
    ang_c = _dft_angles(FNET_GD)
    fc = jnp.arange(FNET_GD, dtype=jnp.int32)
    a_c = ((fc[:, None] * fc[None, :]) % FNET_GD).astype(F32) * (2.0 * np.pi / FNET_GD)
    cs_chan = jnp.concatenate([jnp.cos(a_c), jnp.sin(a_c)], axis=1).astype(BF16)
    pq, h = _fnet_chan(x.reshape(T_LAT, D), ctx.reshape(T_CTX, D), mods, g_norm_mix[0][None, :], cs_chan)
    w_f = w_fnet_out[0].astype(BF16)
    h = _fnet_seq(h, pq, _dft_angles(L), mods, w_f, seq=L, row0=0, mod_row_fn=lambda b: b, name="fnet_seq_lat")
    h = _fnet_seq(h, pq, ang_c, mods, w_f, seq=NCTX, row0=T_LAT, mod_row_fn=lambda b: B, name="fnet_seq_ctx")
    h = ffn(h, 0, last=False)

    mods = layer_mods(1)
    scale = NA_DH ** -0.5 * LOG2E
    w_qkv = jnp.concatenate([w_na_qkv[0][:, :D] * scale, w_na_qkv[0][:, D:]], axis=1).astype(BF16)
    qkv = _modmm(h, mods, g_norm_mix[1][None, :], w_qkv, name="na_qkv")
    o_lat = _na_attend(qkv, _na_bias_table(na_rel_bias[0]))
    o_ctx = _na_ctx_attend(qkv)
    h = _outproj(h, o_lat, o_ctx, mods, w_na_out[0].astype(BF16), n_blocks=T // TM, name="na_out")
    h = ffn(h, 1, last=False)

    mods = layer_mods(2)
    b_s = jnp.broadcast_to(b_sg_spatial[0][:, :, None], (SG_GROUPS, SG_CHUNK, LANES))
    h = _sg(h, mods, g_norm_mix[2][None, :], w_sg_in[0].astype(BF16), g_sg_v[0][None, :],
            w_sg_spatial[0].astype(BF16), b_s, w_sg_out[0].astype(BF16))
    h = ffn(h, 2, last=False)

    mods = layer_mods(3)
    gq = jnp.tile(g_att_q[0] * (ATT_DH ** -0.5 * LOG2E), LANES // ATT_DH)[None, :]
    gk = jnp.tile(g_att_k[0], LANES // ATT_DH)[None, :]
    cos_t, sin_t = _rope_tables()
    q, k, v = _gqa_qkv(h, mods, g_norm_mix[3][None, :], w_att_qkv[0].astype(BF16), gq, gk, cos_t, sin_t)
    o = _gqa_attend(q, k, v)
    h = _outproj(h, o, o, mods, w_att_out[0].astype(BF16), n_blocks=T_LAT // TM, name="gqa_out")
    out = ffn(h, 3, last=True)
    return out.reshape(B, L, D)
```
